```python
import math
import jax, jax.numpy as jnp
from jax import lax
import numpy as np

D_MODEL = 1024
BATCH = 16
SEQ = 2048
DEPTH = 1
DEC_BATCH = 8
DEC_SEQ = 64
PAST_LEN = 4096

CHUNK = 64
Q_BLOCK = 128
N_DIFF_HEADS = 4
DIFF_HEAD_DIM = 64
DIFF_V_DIM = 2 * DIFF_HEAD_DIM
DIFF_WIDTH = N_DIFF_HEADS * DIFF_V_DIM
POOL_WIDTH = D_MODEL - DIFF_WIDTH
POOL_WINDOWS = (2, 4, 8, 16)
N_POOL_GROUPS = len(POOL_WINDOWS)
POOL_GROUP_DIM = POOL_WIDTH // N_POOL_GROUPS
POOL_HIST = max(POOL_WINDOWS) - 1
IN_WIDTH = 3 * DIFF_WIDTH + POOL_WIDTH
N_MEM = 256
N_XA_HEADS = 4
XA_HEAD_DIM = D_MODEL // N_XA_HEADS
N_EXPERTS = 32
TOP_K = 4
D_FF = D_MODEL
SWIGLU_LIMIT = 7.0
SWIGLU_ALPHA = 1.702
MOE_BLOCK = 256
LN_EPS = 1e-5
RMS_EPS = 1e-5
DEEPNORM_ALPHA = (2.0 * DEPTH) ** 0.25
DEEPNORM_BETA = (8.0 * DEPTH) ** -0.25

kernel_name = "hymba_diffattn_pool_moe_streaming_encoder"

F32 = jnp.float32


def layer_norm(x, g, b):
    xf = x.astype(F32)
    mu = jnp.mean(xf, axis=-1, keepdims=True)
    var = jnp.mean(jnp.square(xf - mu), axis=-1, keepdims=True)
    return ((xf - mu) * lax.rsqrt(var + LN_EPS) * g.astype(F32) + b.astype(F32)).astype(x.dtype)


def diff_lambda(lq1, lk1, lq2, lk2, lambda_init):
    return (jnp.exp(jnp.sum(lq1.astype(F32) * lk1.astype(F32)))
            - jnp.exp(jnp.sum(lq2.astype(F32) * lk2.astype(F32))) + lambda_init)


def diff_attend(q, k, v, lam, mask):
    s = jnp.einsum('bqhcd,bkhcd->bhcqk', q, k).astype(F32) * (DIFF_HEAD_DIM ** -0.5)
    if mask is not None:
        s = jnp.where(mask, s, -jnp.inf)
    p = jax.nn.softmax(s, axis=-1)
    a = p[:, :, 0] - lam * p[:, :, 1]
    return jnp.einsum('bhqk,bkhe->bqhe', a.astype(v.dtype), v)


def diff_attention_blocks(q, k, v, lam):
    B, S = q.shape[:2]
    nb = S // Q_BLOCK
    qb = q.reshape(B, nb, Q_BLOCK, N_DIFF_HEADS, 2, DIFF_HEAD_DIM).swapaxes(0, 1)
    key_chunk = jnp.arange(S) // CHUNK

    def one_block(args):
        i, qblk = args
        q_chunk = (i * Q_BLOCK + jnp.arange(Q_BLOCK)) // CHUNK
        mask = key_chunk[None, :] <= q_chunk[:, None]
        return diff_attend(qblk, k, v, lam, mask)

    o = lax.map(one_block, (jnp.arange(nb), qb))
    return o.swapaxes(0, 1).reshape(B, S, N_DIFF_HEADS, DIFF_V_DIM)


def diff_head_norm(o, g, lambda_init):
    B, S = o.shape[:2]
    of = o.astype(F32)
    of = of * lax.rsqrt(jnp.mean(jnp.square(of), axis=-1, keepdims=True) + RMS_EPS)
    of = of * g.astype(F32) * (1.0 - lambda_init)
    return of.reshape(B, S, DIFF_WIDTH).astype(o.dtype)


def pool_mix(u, hist, pos0, pool_w, pool_scale):
    B, S, P = u.shape
    full = jnp.concatenate([hist.astype(u.dtype), u], axis=1)
    uf = full.astype(F32)
    cs = jnp.concatenate([jnp.zeros((B, 1, P), F32), jnp.cumsum(uf, axis=1)], axis=1)
    end = cs[:, POOL_HIST + 1:]
    pos = pos0 + jnp.arange(S)
    outs = []
    for gi, w in enumerate(POOL_WINDOWS):
        sl = slice(gi * POOL_GROUP_DIM, (gi + 1) * POOL_GROUP_DIM)
        start = cs[:, POOL_HIST + 1 - w:POOL_HIST + 1 - w + S, sl]
        cnt = jnp.minimum(w, pos + 1).astype(F32)[None, :, None]
        outs.append((end[..., sl] - start) / cnt - uf[:, POOL_HIST:, sl])
    m = jnp.stack(outs, axis=2)
    y = jnp.einsum('bsgc,gce->bsge', m, pool_w.astype(F32)).reshape(B, S, P) * pool_scale.astype(F32)
    return y.astype(u.dtype), full[:, -POOL_HIST:]


def mem_kv(mem, wk, wv):
    B = mem.shape[0]
    mk = jnp.einsum('bmd,de->bme', mem, wk).reshape(B, N_MEM, N_XA_HEADS, XA_HEAD_DIM)
    mv = jnp.einsum('bmd,de->bme', mem, wv).reshape(B, N_MEM, N_XA_HEADS, XA_HEAD_DIM)
    return mk, mv


def cross_attend(h, mk, mv, wq, wo):
    B, S, D = h.shape
    q = jnp.einsum('bsd,de->bse', h, wq).reshape(B, S, N_XA_HEADS, XA_HEAD_DIM)
    s = jnp.einsum('bshe,bmhe->bhsm', q, mk).astype(F32) * (XA_HEAD_DIM ** -0.5)
    p = jax.nn.softmax(s, axis=-1)
    o = jnp.einsum('bhsm,bmhe->bshe', p.astype(mv.dtype), mv).reshape(B, S, D)
    return jnp.einsum('bsd,de->bse', o, wo)


def clamped_swiglu(g, u):
    g = jnp.minimum(g, SWIGLU_LIMIT)
    u = jnp.clip(u, -SWIGLU_LIMIT, SWIGLU_LIMIT)
    return (u + 1.0) * (g * jax.nn.sigmoid(SWIGLU_ALPHA * g))


def moe_ffn(h, router_w, router_b, w_gate, b_gate, w_up, b_up, w_down, b_down):
    B, S, D = h.shape
    T = B * S
    x = h.reshape(T, D)
    logits = jnp.einsum('td,de->te', x, router_w).astype(F32) + router_b.astype(F32)
    top_val, top_idx = lax.top_k(logits, TOP_K)
    gates = jax.nn.softmax(top_val, axis=-1)
    n_assign = T * TOP_K
    flat_e = top_idx.reshape(-1).astype(jnp.int32)
    flat_tok = jnp.arange(n_assign, dtype=jnp.int32) // TOP_K
    order = jnp.argsort(flat_e)
    sorted_e = flat_e[order]
    counts = jnp.zeros((N_EXPERTS,), jnp.int32).at[flat_e].add(1)
    padded = (counts + MOE_BLOCK - 1) // MOE_BLOCK * MOE_BLOCK
    grp_start = jnp.cumsum(counts) - counts
    pad_end = jnp.cumsum(padded)
    pad_start = pad_end - padded
    dest = pad_start[sorted_e] + (jnp.arange(n_assign, dtype=jnp.int32) - grp_start[sorted_e])
    n_blocks = -(-n_assign // MOE_BLOCK) + N_EXPERTS
    cap = n_blocks * MOE_BLOCK
    row_tok = jnp.full((cap,), T, jnp.int32).at[dest].set(flat_tok[order])
    row_gate = jnp.zeros((cap,), F32).at[dest].set(gates.reshape(-1)[order])
    block_start = jnp.arange(n_blocks, dtype=jnp.int32) * MOE_BLOCK
    block_expert = jnp.minimum(jnp.searchsorted(pad_end, block_start, side='right'), N_EXPERTS - 1)
    x_pad = jnp.concatenate([x, jnp.zeros((1, D), x.dtype)], axis=0)
    xb = x_pad[row_tok].reshape(n_blocks, MOE_BLOCK, D)

    def expert_block(args):
        xblk, e = args
        a = clamped_swiglu(xblk @ w_gate[e] + b_gate[e], xblk @ w_up[e] + b_up[e])
        return a @ w_down[e] + b_down[e]

    yb = lax.map(expert_block, (xb, block_expert)).reshape(cap, D)
    y = jax.ops.segment_sum(yb * row_gate[:, None].astype(yb.dtype), row_tok, num_segments=T + 1)[:T]
    return y.reshape(B, S, D)


def encoder_layer(x, mk, mv, past_k, past_v, pool_hist, pos0, lw, lambda_init):
    B, S, _ = x.shape
    z = jnp.einsum('bsd,de->bse', x, lw['w_in'])
    q = z[..., :DIFF_WIDTH].reshape(B, S, N_DIFF_HEADS, 2, DIFF_HEAD_DIM)
    k = z[..., DIFF_WIDTH:2 * DIFF_WIDTH].reshape(B, S, N_DIFF_HEADS, 2, DIFF_HEAD_DIM)
    v = z[..., 2 * DIFF_WIDTH:3 * DIFF_WIDTH].reshape(B, S, N_DIFF_HEADS, DIFF_V_DIM)
    u = z[..., 3 * DIFF_WIDTH:]
    lam = diff_lambda(lw['lambda_q1'], lw['lambda_k1'], lw['lambda_q2'], lw['lambda_k2'], lambda_init)
    if past_k is None:
        o = diff_attention_blocks(q, k, v, lam)
        pool_hist = jnp.zeros((B, POOL_HIST, POOL_WIDTH), u.dtype)
    else:
        k_all = jnp.concatenate([past_k.reshape(B, -1, N_DIFF_HEADS, 2, DIFF_HEAD_DIM).astype(k.dtype), k], axis=1)
        v_all = jnp.concatenate([past_v.astype(v.dtype), v], axis=1)
        o = diff_attend(q, k_all, v_all, lam, None)
    diff_out = diff_head_norm(o, lw['subln_g'], lambda_init)
    pool_out, new_pool = pool_mix(u, pool_hist, pos0, lw['pool_w'], lw['pool_scale'])
    mix = jnp.einsum('bse,ed->bsd', jnp.concatenate([diff_out, pool_out], axis=-1), lw['w_out'])
    h1 = layer_norm(DEEPNORM_ALPHA * x + mix, lw['ln1_g'], lw['ln1_b'])
    h2 = layer_norm(DEEPNORM_ALPHA * h1 + cross_attend(h1, mk, mv, lw['xa_wq'], lw['xa_wo']),
                    lw['ln2_g'], lw['ln2_b'])
    ff = moe_ffn(h2, lw['router_w'], lw['router_b'], lw['moe_w_gate'], lw['moe_b_gate'],
                 lw['moe_w_up'], lw['moe_b_up'], lw['moe_w_down'], lw['moe_b_down'])
    y = layer_norm(DEEPNORM_ALPHA * h2 + ff, lw['ln3_g'], lw['ln3_b'])
    new_k = k.reshape(B, S, N_DIFF_HEADS, DIFF_V_DIM)
    return y, new_k, v, new_pool


def setup_inputs(seed: int = 0) -> dict:
    key = jax.random.key(seed)
    ks = iter(jax.random.split(key, 48))
    nrm = lambda shape, s: jax.random.normal(next(ks), shape, F32) * s
    D, L = D_MODEL, DEPTH
    w_in = nrm((L, D, IN_WIDTH), D ** -0.5)
    v_scale = jnp.ones((IN_WIDTH,), F32).at[2 * DIFF_WIDTH:3 * DIFF_WIDTH].set(DEEPNORM_BETA)
    w_in = w_in * v_scale
    return {
        "x_prompt": nrm((BATCH, SEQ, D), 1.0),
        "x_sample": nrm((DEC_BATCH, DEC_SEQ, D), 1.0),
        "mem_prompt": nrm((BATCH, N_MEM, D), 1.0),
        "cache_diff_k": nrm((L, DEC_BATCH, PAST_LEN, N_DIFF_HEADS, DIFF_V_DIM), 1.0),
        "cache_diff_v": nrm((L, DEC_BATCH, PAST_LEN, N_DIFF_HEADS, DIFF_V_DIM), DEEPNORM_BETA),
        "state_pool": nrm((L, DEC_BATCH, POOL_HIST, POOL_WIDTH), 1.0),
        "cache_mem_k": nrm((L, DEC_BATCH, N_MEM, N_XA_HEADS, XA_HEAD_DIM), 1.0),
        "cache_mem_v": nrm((L, DEC_BATCH, N_MEM, N_XA_HEADS, XA_HEAD_DIM), DEEPNORM_BETA),
        "w_in": w_in,
        "lambda_q1": nrm((L, DIFF_HEAD_DIM), 0.1),
        "lambda_k1": nrm((L, DIFF_HEAD_DIM), 0.1),
        "lambda_q2": nrm((L, DIFF_HEAD_DIM), 0.1),
        "lambda_k2": nrm((L, DIFF_HEAD_DIM), 0.1),
        "subln_g": 1.0 + nrm((L, DIFF_V_DIM), 0.02),
        "pool_w": nrm((L, N_POOL_GROUPS, POOL_GROUP_DIM, POOL_GROUP_DIM), POOL_GROUP_DIM ** -0.5),
        "pool_scale": 1.0 + nrm((L, POOL_WIDTH), 0.1),
        "w_out": nrm((L, D, D), D ** -0.5 * DEEPNORM_BETA),
        "ln1_g": 1.0 + nrm((L, D), 0.02),
        "ln1_b": nrm((L, D), 0.02),
        "xa_wq": nrm((L, D, D), D ** -0.5),
        "xa_wk": nrm((L, D, D), D ** -0.5),
        "xa_wv": nrm((L, D, D), D ** -0.5 * DEEPNORM_BETA),
        "xa_wo": nrm((L, D, D), D ** -0.5 * DEEPNORM_BETA),
        "ln2_g": 1.0 + nrm((L, D), 0.02),
        "ln2_b": nrm((L, D), 0.02),
        "router_w": nrm((L, D, N_EXPERTS), D ** -0.5),
        "router_b": nrm((L, N_EXPERTS), 0.01),
        "moe_w_gate": nrm((L, N_EXPERTS, D, D_FF), D ** -0.5 * DEEPNORM_BETA),
        "moe_b_gate": nrm((L, N_EXPERTS, D_FF), 0.01),
        "moe_w_up": nrm((L, N_EXPERTS, D, D_FF), D ** -0.5 * DEEPNORM_BETA),
        "moe_b_up": nrm((L, N_EXPERTS, D_FF), 0.01),
        "moe_w_down": nrm((L, N_EXPERTS, D_FF, D), D_FF ** -0.5 * DEEPNORM_BETA),
        "moe_b_down": nrm((L, N_EXPERTS, D), 0.01),
        "ln3_g": 1.0 + nrm((L, D), 0.02),
        "ln3_b": nrm((L, D), 0.02),
    }


def reference(x_prompt, x_sample, mem_prompt, cache_diff_k, cache_diff_v, state_pool, cache_mem_k, cache_mem_v,
              w_in, lambda_q1, lambda_k1, lambda_q2, lambda_k2, subln_g, pool_w, pool_scale, w_out,
              ln1_g, ln1_b, xa_wq, xa_wk, xa_wv, xa_wo, ln2_g, ln2_b, router_w, router_b,
              moe_w_gate, moe_b_gate, moe_w_up, moe_b_up, moe_w_down, moe_b_down, ln3_g, ln3_b):
    xp, xs = x_prompt, x_sample
    p_k, p_v, p_pool, p_mk, p_mv = [], [], [], [], []
    s_k, s_v, s_pool = [], [], []
    for l in range(DEPTH):
        lambda_init = 0.8 - 0.6 * math.exp(-0.3 * l)
        lw = dict(w_in=w_in[l], lambda_q1=lambda_q1[l], lambda_k1=lambda_k1[l], lambda_q2=lambda_q2[l],
                  lambda_k2=lambda_k2[l], subln_g=subln_g[l], pool_w=pool_w[l], pool_scale=pool_scale[l],
                  w_out=w_out[l], ln1_g=ln1_g[l], ln1_b=ln1_b[l], xa_wq=xa_wq[l], xa_wo=xa_wo[l],
                  ln2_g=ln2_g[l], ln2_b=ln2_b[l], router_w=router_w[l], router_b=router_b[l],
                  moe_w_gate=moe_w_gate[l], moe_b_gate=moe_b_gate[l], moe_w_up=moe_w_up[l],
                  moe_b_up=moe_b_up[l], moe_w_down=moe_w_down[l], moe_b_down=moe_b_down[l],
                  ln3_g=ln3_g[l], ln3_b=ln3_b[l])
        mk, mv = mem_kv(mem_prompt, xa_wk[l], xa_wv[l])
        xp, nk, nv, npool = encoder_layer(xp, mk, mv, None, None, None, 0, lw, lambda_init)
        p_k.append(nk); p_v.append(nv); p_pool.append(npool); p_mk.append(mk); p_mv.append(mv)
        xs, sk, sv, spool = encoder_layer(xs, cache_mem_k[l], cache_mem_v[l], cache_diff_k[l], cache_diff_v[l],
                                          state_pool[l], PAST_LEN, lw, lambda_init)
        s_k.append(sk); s_v.append(sv); s_pool.append(spool)
    return (xp, xs, jnp.stack(p_k), jnp.stack(p_v), jnp.stack(p_pool), jnp.stack(p_mk), jnp.stack(p_mv),
            jnp.stack(s_k), jnp.stack(s_v), jnp.stack(s_pool))
```

```python
import functools
import math

import jax
import jax.numpy as jnp
from jax import lax
from jax.experimental import pallas as pl
from jax.experimental.pallas import tpu as pltpu

F32 = jnp.float32
BF16 = jnp.bfloat16
I32 = jnp.int32
U32 = jnp.uint32

D_MODEL = 1024
CHUNK = 64
N_DIFF_HEADS = 4
DIFF_HEAD_DIM = 64
DIFF_V_DIM = 2 * DIFF_HEAD_DIM
DIFF_WIDTH = N_DIFF_HEADS * DIFF_V_DIM
POOL_WIDTH = D_MODEL - DIFF_WIDTH
POOL_WINDOWS = (2, 4, 8, 16)
POOL_GROUP_DIM = POOL_WIDTH // len(POOL_WINDOWS)
POOL_HIST = max(POOL_WINDOWS) - 1
HIST_ROWS = POOL_HIST + 1
IN_WIDTH = 3 * DIFF_WIDTH + POOL_WIDTH
N_MEM = 256
N_XA_HEADS = 4
XA_HEAD_DIM = D_MODEL // N_XA_HEADS
N_EXPERTS = 32
TOP_K = 4
SWIGLU_LIMIT = 7.0
SWIGLU_ALPHA = 1.702
MOE_BLOCK = 256
LN_EPS = 1e-5
RMS_EPS = 1e-5
DEPTH = 1
DEEPNORM_ALPHA = (2.0 * DEPTH) ** 0.25
NEG_BIG = -1e30

VMEM_LIMIT = 56 * 1024 * 1024

TM_PROJ = 512
TQ_ATT = 256
TK_ATT = 256
TK_PAST = 1024
TM_POST = 512
TR_RANK = 512
TM_MOE = 256


def _cparams(*sem):
    return pltpu.CompilerParams(dimension_semantics=sem, vmem_limit_bytes=VMEM_LIMIT)


def _layer_norm(x, g, b):
    mu = jnp.mean(x, axis=-1, keepdims=True)
    xc = x - mu
    var = jnp.mean(xc * xc, axis=-1, keepdims=True)
    return xc * lax.rsqrt(var + LN_EPS) * g + b


def _inproj_kernel(x_ref, w_ref, q_ref, k_ref, v_ref, u_ref, kb_ref, vb_ref):
    x = x_ref[...].astype(BF16)
    w = DIFF_WIDTH
    q = jnp.dot(x, w_ref[:, 0:w], preferred_element_type=F32)
    q_ref[...] = (q * (DIFF_HEAD_DIM ** -0.5)).astype(BF16)
    k = jnp.dot(x, w_ref[:, w:2 * w], preferred_element_type=F32)
    k_ref[...] = k
    kb_ref[...] = k.astype(BF16)
    v = jnp.dot(x, w_ref[:, 2 * w:3 * w], preferred_element_type=F32)
    v_ref[...] = v
    vb_ref[...] = v.astype(BF16)
    u_ref[...] = jnp.dot(x, w_ref[:, 3 * w:], preferred_element_type=F32)


def _inproj(x2d, w_in_bf, tm):
    t = x2d.shape[0]
    w = DIFF_WIDTH
    row = lambda i: (i, 0)
    out_spec = pl.BlockSpec((tm, w), row)
    return pl.pallas_call(
        _inproj_kernel,
        grid=(t // tm,),
        in_specs=[pl.BlockSpec((tm, D_MODEL), row),
                  pl.BlockSpec((D_MODEL, IN_WIDTH), lambda i: (0, 0))],
        out_specs=[out_spec] * 6,
        out_shape=[jax.ShapeDtypeStruct((t, w), BF16), jax.ShapeDtypeStruct((t, w), F32),
                   jax.ShapeDtypeStruct((t, w), F32), jax.ShapeDtypeStruct((t, w), F32),
                   jax.ShapeDtypeStruct((t, w), BF16), jax.ShapeDtypeStruct((t, w), BF16)],
        compiler_params=_cparams("parallel"),
        name="inproj",
    )(x2d, w_in_bf)


def _memkv_kernel(x_ref, wk_ref, wv_ref, k_ref, v_ref, kb_ref, vb_ref):
    x = x_ref[...].astype(BF16)
    k = jnp.dot(x, wk_ref[...], preferred_element_type=F32)
    k_ref[...] = k
    kb_ref[...] = k.astype(BF16)
    v = jnp.dot(x, wv_ref[...], preferred_element_type=F32)
    v_ref[...] = v
    vb_ref[...] = v.astype(BF16)


def _memkv(mem2d, wk_bf, wv_bf, tm):
    t = mem2d.shape[0]
    row = lambda i: (i, 0)
    full = lambda i: (0, 0)
    spec = pl.BlockSpec((tm, D_MODEL), row)
    return pl.pallas_call(
        _memkv_kernel,
        grid=(t // tm,),
        in_specs=[spec, pl.BlockSpec((D_MODEL, D_MODEL), full), pl.BlockSpec((D_MODEL, D_MODEL), full)],
        out_specs=[spec] * 4,
        out_shape=[jax.ShapeDtypeStruct((t, D_MODEL), F32), jax.ShapeDtypeStruct((t, D_MODEL), F32),
                   jax.ShapeDtypeStruct((t, D_MODEL), BF16), jax.ShapeDtypeStruct((t, D_MODEL), BF16)],
        compiler_params=_cparams("parallel"),
        name="memkv",
    )(mem2d, wk_bf, wv_bf)


def _diff_lambda(lq1_ref, lk1_ref, lq2_ref, lk2_ref, lambda_init):
    s1 = jnp.sum(lq1_ref[...] * lk1_ref[...], axis=-1, keepdims=True)
    s2 = jnp.sum(lq2_ref[...] * lk2_ref[...], axis=-1, keepdims=True)
    return jnp.exp(s1) - jnp.exp(s2) + lambda_init


def _stack_maps(qh):
    lane = lax.broadcasted_iota(I32, qh.shape, 1)
    zero = jnp.zeros_like(qh)
    return jnp.concatenate([jnp.where(lane < DIFF_HEAD_DIM, qh, zero),
                            jnp.where(lane >= DIFF_HEAD_DIM, qh, zero)], axis=0)


def _softmax_step(qq, kh, vh, m, l, acc, mask):
    s = lax.dot_general(qq, kh, (((1,), (1,)), ((), ())), preferred_element_type=F32)
    if mask is not None:
        s = jnp.where(mask, s, NEG_BIG)
    m_new = jnp.maximum(m, jnp.max(s, axis=-1, keepdims=True))
    a = jnp.exp(m - m_new)
    p = jnp.exp(s - m_new)
    l = a * l + jnp.sum(p, axis=-1, keepdims=True)
    acc = a * acc + jnp.dot(p.astype(BF16), vh, preferred_element_type=F32)
    return m_new, l, acc


def _diff_finish(l, acc, lam, g, lambda_init, tq):
    o = acc[:tq] / l[:tq] - lam * (acc[tq:] / l[tq:])
    o = o * lax.rsqrt(jnp.mean(o * o, axis=-1, keepdims=True) + RMS_EPS)
    return o * g * (1.0 - lambda_init)


def _diff_prompt_kernel(q_ref, k_ref, v_ref, lq1_ref, lk1_ref, lq2_ref, lk2_ref, g_ref, o_ref, *,
                        lambda_init):
    i = pl.program_id(1)
    tq, tk, hd = TQ_ATT, TK_ATT, DIFF_V_DIM
    lam = _diff_lambda(lq1_ref, lk1_ref, lq2_ref, lk2_ref, lambda_init)
    g = g_ref[...]
    qrow = lax.broadcasted_iota(I32, (2 * tq, tk), 0)
    qrow = jnp.where(qrow >= tq, qrow - tq, qrow)
    kcol = lax.broadcasted_iota(I32, (2 * tq, tk), 1)
    diag_mask = (kcol // CHUNK) <= (qrow // CHUNK)
    for h in range(N_DIFF_HEADS):
        cs = slice(h * hd, (h + 1) * hd)
        qq = _stack_maps(q_ref[:, cs])

        def body(j, carry):
            r0 = pl.multiple_of(j * tk, tk)
            return _softmax_step(qq, k_ref[pl.ds(r0, tk), cs], v_ref[pl.ds(r0, tk), cs], *carry, None)

        init = (jnp.full((2 * tq, 1), NEG_BIG, F32), jnp.zeros((2 * tq, 1), F32),
                jnp.zeros((2 * tq, hd), F32))
        carry = lax.fori_loop(0, i, body, init)
        r0 = pl.multiple_of(i * tk, tk)
        _, l, acc = _softmax_step(qq, k_ref[pl.ds(r0, tk), cs], v_ref[pl.ds(r0, tk), cs], *carry, diag_mask)
        o_ref[:, cs] = _diff_finish(l, acc, lam, g, lambda_init, tq).astype(o_ref.dtype)


def _diff_prompt(qb, kb, vb, lam_vecs, subln_g, batch, seq, lambda_init):
    assert TQ_ATT == TK_ATT and TQ_ATT % CHUNK == 0 and seq % TQ_ATT == 0
    nq = seq // TQ_ATT
    w = DIFF_WIDTH
    small = lambda b, i: (0, 0)
    return pl.pallas_call(
        functools.partial(_diff_prompt_kernel, lambda_init=lambda_init),
        grid=(batch, nq),
        in_specs=[pl.BlockSpec((TQ_ATT, w), lambda b, i: (b * nq + i, 0)),
                  pl.BlockSpec((seq, w), lambda b, i: (b, 0)),
                  pl.BlockSpec((seq, w), lambda b, i: (b, 0))]
                 + [pl.BlockSpec((1, DIFF_HEAD_DIM), small)] * 4
                 + [pl.BlockSpec((1, DIFF_V_DIM), small)],
        out_specs=pl.BlockSpec((TQ_ATT, w), lambda b, i: (b * nq + i, 0)),
        out_shape=jax.ShapeDtypeStruct((batch * seq, w), BF16),
        compiler_params=_cparams("parallel", "arbitrary"),
        name="diff_attn_prompt",
    )(qb, kb, vb, *lam_vecs, subln_g)


def _diff_sample_kernel(q_ref, pk_ref, pv_ref, nk_ref, nv_ref, lq1_ref, lk1_ref, lq2_ref, lk2_ref, g_ref,
                        o_ref, m_ref, l_ref, acc_ref, *, lambda_init, n_past):
    j = pl.program_id(1)
    tq, hd = q_ref.shape[0], DIFF_V_DIM

    @pl.when(j == 0)
    def _():
        m_ref[...] = jnp.full(m_ref.shape, NEG_BIG, F32)
        l_ref[...] = jnp.zeros(l_ref.shape, F32)
        acc_ref[...] = jnp.zeros(acc_ref.shape, F32)

    def update(k_of, v_of):
        for h in range(N_DIFF_HEADS):
            cs = slice(h * hd, (h + 1) * hd)
            qq = _stack_maps(q_ref[:, cs])
            m, l, acc = _softmax_step(qq, k_of(cs), v_of(cs), m_ref[h], l_ref[h], acc_ref[h], None)
            m_ref[h] = m
            l_ref[h] = l
            acc_ref[h] = acc

    @pl.when(j < n_past)
    def _():
        update(lambda cs: pk_ref[0, :, cs].astype(BF16), lambda cs: pv_ref[0, :, cs].astype(BF16))

    @pl.when(j == n_past)
    def _():
        update(lambda cs: nk_ref[:, cs], lambda cs: nv_ref[:, cs])
        lam = _diff_lambda(lq1_ref, lk1_ref, lq2_ref, lk2_ref, lambda_init)
        for h in range(N_DIFF_HEADS):
            cs = slice(h * hd, (h + 1) * hd)
            o_ref[:, cs] = _diff_finish(l_ref[h], acc_ref[h], lam, g_ref[...], lambda_init, tq).astype(o_ref.dtype)


def _diff_sample(qb, past_k, past_v, kb, vb, lam_vecs, subln_g, batch, seq, lambda_init):
    past_len = past_k.shape[1]
    assert past_len % TK_PAST == 0
    n_past = past_len // TK_PAST
    w = DIFF_WIDTH
    small = lambda b, j: (0, 0)
    past_spec = pl.BlockSpec((1, TK_PAST, w), lambda b, j: (b, jnp.minimum(j, n_past - 1), 0))
    new_spec = pl.BlockSpec((seq, w), lambda b, j: (b, 0))
    return pl.pallas_call(
        functools.partial(_diff_sample_kernel, lambda_init=lambda_init, n_past=n_past),
        grid=(batch, n_past + 1),
        in_specs=[new_spec, past_spec, past_spec, new_spec, new_spec]
                 + [pl.BlockSpec((1, DIFF_HEAD_DIM), small)] * 4
                 + [pl.BlockSpec((1, DIFF_V_DIM), small)],
        out_specs=new_spec,
        out_shape=jax.ShapeDtypeStruct((batch * seq, w), BF16),
        scratch_shapes=[pltpu.VMEM((N_DIFF_HEADS, 2 * seq, 1), F32),
                        pltpu.VMEM((N_DIFF_HEADS, 2 * seq, 1), F32),
                        pltpu.VMEM((N_DIFF_HEADS, 2 * seq, DIFF_V_DIM), F32)],
        compiler_params=_cparams("parallel", "arbitrary"),
        name="diff_attn_sample",
    )(qb, past_k, past_v, kb, vb, *lam_vecs, subln_g)


def _post_kernel(d_ref, u_ref, hist_ref, x_ref, pw_ref, ps_ref, wo1_ref, g1_ref, b1_ref, wq_ref, mk_ref,
                 mv_ref, wo2_ref, g2_ref, b2_ref, rwh_ref, rwl_ref, rb_ref,
                 h2_ref, h2p_ref, idx_ref, gate_ref, full_ref, *, tm, seq, pos0, zero_first_hist):
    i = pl.program_id(0)
    row0 = (i * tm) % seq

    hist = hist_ref[...]
    if zero_first_hist:
        hist = jnp.where(row0 == 0, jnp.zeros_like(hist), hist)
    full_ref[0:HIST_ROWS, :] = hist
    u = u_ref[...]
    full_ref[HIST_ROWS:, :] = u
    pos = pos0 + row0 + lax.broadcasted_iota(I32, (tm, 1), 0)
    pooled = []
    for gi, w in enumerate(POOL_WINDOWS):
        cs = slice(gi * POOL_GROUP_DIM, (gi + 1) * POOL_GROUP_DIM)
        acc = u[:, cs]
        for back in range(1, w):
            acc = acc + full_ref[HIST_ROWS - back:HIST_ROWS - back + tm, cs]
        cnt = jnp.minimum(w, pos + 1).astype(F32)
        m = acc / cnt - u[:, cs]
        y = jnp.dot(m.astype(BF16), pw_ref[gi], preferred_element_type=F32)
        pooled.append((y * ps_ref[:, cs]).astype(BF16))
    mixed_in = jnp.concatenate([d_ref[...]] + pooled, axis=-1)

    mix = jnp.dot(mixed_in, wo1_ref[...], preferred_element_type=F32)
    h1 = _layer_norm(DEEPNORM_ALPHA * x_ref[...] + mix, g1_ref[...], b1_ref[...])

    q = jnp.dot(h1.astype(BF16), wq_ref[...], preferred_element_type=F32)
    qb = (q * (XA_HEAD_DIM ** -0.5)).astype(BF16)
    heads = []
    for h in range(N_XA_HEADS):
        cs = slice(h * XA_HEAD_DIM, (h + 1) * XA_HEAD_DIM)
        s = lax.dot_general(qb[:, cs], mk_ref[:, cs], (((1,), (1,)), ((), ())), preferred_element_type=F32)
        e = jnp.exp(s - jnp.max(s, axis=-1, keepdims=True))
        p = e / jnp.sum(e, axis=-1, keepdims=True)
        heads.append(jnp.dot(p.astype(BF16), mv_ref[:, cs], preferred_element_type=F32).astype(BF16))
    ca = jnp.dot(jnp.concatenate(heads, axis=-1), wo2_ref[...], preferred_element_type=F32)
    h2 = _layer_norm(DEEPNORM_ALPHA * h1 + ca, g2_ref[...], b2_ref[...])
    h2_ref[...] = h2

    hb = h2.astype(BF16)
    bits = pltpu.bitcast(hb.astype(F32), U32)
    half = D_MODEL // 2
    h2p_ref[...] = (bits[:, :half] >> 16) | (bits[:, half:] & jnp.uint32(0xFFFF0000))

    hl = (h2 - hb.astype(F32)).astype(BF16)
    dn = (((1,), (1,)), ((), ()))
    logits = (lax.dot_general(rwh_ref[...], hb, dn, preferred_element_type=F32)
              + lax.dot_general(rwh_ref[...], hl, dn, preferred_element_type=F32)
              + lax.dot_general(rwl_ref[...], hb, dn, preferred_element_type=F32)
              + rb_ref[...])
    erow = lax.broadcasted_iota(I32, logits.shape, 0)
    vals, idxs = [], []
    for _ in range(TOP_K):
        mx = jnp.max(logits, axis=0, keepdims=True)
        ix = jnp.min(jnp.where(logits == mx, erow, N_EXPERTS), axis=0, keepdims=True)
        vals.append(mx)
        idxs.append(ix)
        logits = jnp.where(erow == ix, -jnp.inf, logits)
    ex = [jnp.exp(v - vals[0]) for v in vals]
    den = ex[0] + ex[1] + ex[2] + ex[3]
    idx_ref[0] = jnp.concatenate(idxs, axis=0)
    gate_ref[0] = jnp.concatenate([e / den for e in ex], axis=0)


def _post(diff_out, u, hist_arr, hist_map, x2d, mkb, mvb, wts, *, tm, seq, pos0, zero_first_hist):
    t = x2d.shape[0]
    n = t // tm
    assert seq % tm == 0 or tm % seq == 0
    row = lambda i: (i, 0)
    full2 = lambda i: (0, 0)
    mem_map = lambda i: ((i * tm) // seq, 0)
    vec = pl.BlockSpec((1, D_MODEL), full2)
    wspec = pl.BlockSpec((D_MODEL, D_MODEL), full2)
    in_specs = [
        pl.BlockSpec((tm, DIFF_WIDTH), row), pl.BlockSpec((tm, POOL_WIDTH), row),
        pl.BlockSpec((HIST_ROWS, POOL_WIDTH), hist_map), pl.BlockSpec((tm, D_MODEL), row),
        pl.BlockSpec((len(POOL_WINDOWS), POOL_GROUP_DIM, POOL_GROUP_DIM), lambda i: (0, 0, 0)),
        pl.BlockSpec((1, POOL_WIDTH), full2),
        wspec, vec, vec,
        wspec, pl.BlockSpec((N_MEM, D_MODEL), mem_map), pl.BlockSpec((N_MEM, D_MODEL), mem_map),
        wspec, vec, vec,
        pl.BlockSpec((N_EXPERTS, D_MODEL), full2), pl.BlockSpec((N_EXPERTS, D_MODEL), full2),
        pl.BlockSpec((N_EXPERTS, 1), full2),
    ]
    out_specs = [pl.BlockSpec((tm, D_MODEL), row), pl.BlockSpec((tm, D_MODEL // 2), row),
                 pl.BlockSpec((1, TOP_K, tm), lambda i: (i, 0, 0)),
                 pl.BlockSpec((1, TOP_K, tm), lambda i: (i, 0, 0))]
    out_shape = [jax.ShapeDtypeStruct((t, D_MODEL), F32), jax.ShapeDtypeStruct((t, D_MODEL // 2), U32),
                 jax.ShapeDtypeStruct((n, TOP_K, tm), I32), jax.ShapeDtypeStruct((n, TOP_K, tm), F32)]
    return pl.pallas_call(
        functools.partial(_post_kernel, tm=tm, seq=seq, pos0=pos0, zero_first_hist=zero_first_hist),
        grid=(n,),
        in_specs=in_specs,
        out_specs=out_specs,
        out_shape=out_shape,
        scratch_shapes=[pltpu.VMEM((HIST_ROWS + tm, POOL_WIDTH), F32)],
        compiler_params=_cparams("parallel"),
        name="post_attn",
    )(diff_out, u, hist_arr, x2d, wts["pool_w"], wts["pool_scale"], wts["w_out"], wts["ln1_g"], wts["ln1_b"],
      wts["xa_wq"], mkb, mvb, wts["xa_wo"], wts["ln2_g"], wts["ln2_b"], wts["rw_hi"], wts["rw_lo"],
      wts["router_b"])


def _rank_kernel(idx_ref, rank_ref, cnt_ref, carry_ref):
    @pl.when(pl.program_id(0) == 0)
    def _():
        carry_ref[...] = jnp.zeros(carry_ref.shape, F32)

    idx = idx_ref[...]
    tr = idx.shape[1]
    erow = lax.broadcasted_iota(I32, (N_EXPERTS, tr), 0)
    hits = [erow == idx[k:k + 1, :] for k in range(TOP_K)]
    onehot = sum(h.astype(F32) for h in hits)
    earlier = (lax.broadcasted_iota(I32, (tr, tr), 0) < lax.broadcasted_iota(I32, (tr, tr), 1)).astype(BF16)
    before = jnp.dot(onehot.astype(BF16), earlier, preferred_element_type=F32) + carry_ref[:, 0:1]
    ranks = [jnp.sum(jnp.where(h, before, 0.0), axis=0, keepdims=True) for h in hits]
    rank_ref[...] = jnp.concatenate(ranks, axis=0).astype(I32)
    carry_ref[...] = carry_ref[...] + jnp.sum(onehot, axis=1, keepdims=True)
    cnt_ref[...] = carry_ref[...].astype(I32)


def _ranks(idx_all):
    t = idx_all.shape[1]
    assert t % TR_RANK == 0
    return pl.pallas_call(
        _rank_kernel,
        grid=(t // TR_RANK,),
        in_specs=[pl.BlockSpec((TOP_K, TR_RANK), lambda i: (0, i))],
        out_specs=[pl.BlockSpec((TOP_K, TR_RANK), lambda i: (0, i)),
                   pl.BlockSpec((N_EXPERTS, 128), lambda i: (0, 0))],
        out_shape=[jax.ShapeDtypeStruct((TOP_K, t), I32), jax.ShapeDtypeStruct((N_EXPERTS, 128), I32)],
        scratch_shapes=[pltpu.VMEM((N_EXPERTS, 128), F32)],
        compiler_params=_cparams("arbitrary"),
        name="moe_rank",
    )(idx_all)


def _dest_kernel(start_ref, idx_ref, rank_ref, dest_ref):
    idx = idx_ref[...]
    base = jnp.zeros(idx.shape, I32)
    for e in range(N_EXPERTS):
        base = jnp.where(idx == e, start_ref[e], base)
    dest_ref[...] = base + rank_ref[...]


def _dests(pad_start, idx_all, rank_all):
    t = idx_all.shape[1]
    spec = pl.BlockSpec((TOP_K, TR_RANK), lambda i, s: (0, i))
    return pl.pallas_call(
        _dest_kernel,
        grid_spec=pltpu.PrefetchScalarGridSpec(
            num_scalar_prefetch=1, grid=(t // TR_RANK,), in_specs=[spec, spec], out_specs=spec),
        out_shape=jax.ShapeDtypeStruct((TOP_K, t), I32),
        compiler_params=_cparams("parallel"),
        name="moe_dest",
    )(pad_start, idx_all, rank_all)


def _row_copy(src_ref, src_row, dst_ref, dst_row, sem):
    return pltpu.make_async_copy(src_ref.at[pl.ds(src_row, 1)], dst_ref.at[pl.ds(dst_row, 1)], sem)


def _dispatch_kernel(hp_ref, hs_ref, dest_hbm, xs_in, xs_out, dest_smem, sem_idx, sem_rows, *, n_prompt):
    del xs_in
    i = pl.program_id(0)
    n = TM_MOE * TOP_K
    cp = pltpu.make_async_copy(dest_hbm.at[pl.ds(i * n, n)], dest_smem, sem_idx)
    cp.start()
    cp.wait()

    def scatter_from(src_ref):
        def issue(t, c):
            for k in range(TOP_K):
                _row_copy(src_ref, t, xs_out, dest_smem[t * TOP_K + k], sem_rows).start()
            return c

        lax.fori_loop(0, TM_MOE, issue, 0)

        def drain(t, c):
            for k in range(TOP_K):
                _row_copy(src_ref, 0, xs_out, 0, sem_rows).wait()
            return c

        lax.fori_loop(0, TM_MOE, drain, 0)

    @pl.when(i < n_prompt)
    def _():
        scatter_from(hp_ref)

    @pl.when(i >= n_prompt)
    def _():
        scatter_from(hs_ref)


def _dispatch(h2p_p, h2p_s, dest_flat, cap):
    n_prompt = h2p_p.shape[0] // TM_MOE
    n_sample = h2p_s.shape[0] // TM_MOE
    half = D_MODEL // 2
    xs0 = jnp.zeros((cap, half), U32)
    return pl.pallas_call(
        functools.partial(_dispatch_kernel, n_prompt=n_prompt),
        grid=(n_prompt + n_sample,),
        in_specs=[pl.BlockSpec((TM_MOE, half), lambda i: (jnp.minimum(i, n_prompt - 1), 0)),
                  pl.BlockSpec((TM_MOE, half), lambda i: (jnp.maximum(i - n_prompt, 0), 0)),
                  pl.BlockSpec(memory_space=pl.ANY), pl.BlockSpec(memory_space=pl.ANY)],
        out_specs=pl.BlockSpec(memory_space=pl.ANY),
        out_shape=jax.ShapeDtypeStruct((cap, half), U32),
        scratch_shapes=[pltpu.SMEM((TM_MOE * TOP_K,), I32), pltpu.SemaphoreType.DMA, pltpu.SemaphoreType.DMA],
        input_output_aliases={3: 0},
        compiler_params=_cparams("arbitrary"),
        name="moe_dispatch",
    )(h2p_p, h2p_s, dest_flat, xs0)


def _expert_kernel(be_ref, act_ref, x_ref, wg_ref, bg_ref, wu_ref, bu_ref, wd_ref, bd_ref, o_ref):
    b = pl.program_id(0)

    @pl.when(act_ref[b] == 1)
    def _():
        p = x_ref[...]
        lo = pltpu.bitcast(p << 16, F32)
        hi = pltpu.bitcast(p & jnp.uint32(0xFFFF0000), F32)
        x = jnp.concatenate([lo, hi], axis=-1).astype(BF16)
        g = jnp.dot(x, wg_ref[0], preferred_element_type=F32) + bg_ref[0]
        u = jnp.dot(x, wu_ref[0], preferred_element_type=F32) + bu_ref[0]
        g = jnp.minimum(g, SWIGLU_LIMIT)
        u = jnp.clip(u, -SWIGLU_LIMIT, SWIGLU_LIMIT)
        a = (u + 1.0) * (g * jax.nn.sigmoid(SWIGLU_ALPHA * g))
        o_ref[...] = jnp.dot(a.astype(BF16), wd_ref[0], preferred_element_type=F32) + bd_ref[0]

    @pl.when(act_ref[b] == 0)
    def _():
        o_ref[...] = jnp.zeros(o_ref.shape, F32)


def _experts(block_expert, block_active, xs, wg, bg, wu, bu, wd, bd):
    cap = xs.shape[0]
    n_blocks = cap // MOE_BLOCK
    wspec = pl.BlockSpec((1, D_MODEL, D_MODEL), lambda b, be, act: (be[b], 0, 0))
    bspec = pl.BlockSpec((1, 1, D_MODEL), lambda b, be, act: (be[b], 0, 0))
    return pl.pallas_call(
        _expert_kernel,
        grid_spec=pltpu.PrefetchScalarGridSpec(
            num_scalar_prefetch=2, grid=(n_blocks,),
            in_specs=[pl.BlockSpec((MOE_BLOCK, D_MODEL // 2), lambda b, be, act: (b, 0)),
                      wspec, bspec, wspec, bspec, wspec, bspec],
            out_specs=pl.BlockSpec((MOE_BLOCK, D_MODEL), lambda b, be, act: (b, 0))),
        out_shape=jax.ShapeDtypeStruct((cap, D_MODEL), F32),
        compiler_params=_cparams("arbitrary"),
        name="moe_experts",
    )(block_expert, block_active, xs, wg, bg, wu, bu, wd, bd)


def _combine_kernel(hp_ref, hs_ref, gate_ref, dest_hbm, yb_hbm, g_ref, b_ref, yp_ref, ys_ref,
                    dest_smem, rows_ref, sem_idx, sem_rows, *, n_prompt):
    i = pl.program_id(0)
    n = TM_MOE * TOP_K
    cp = pltpu.make_async_copy(dest_hbm.at[pl.ds(i * n, n)], dest_smem, sem_idx)
    cp.start()
    cp.wait()

    def issue(t, c):
        for k in range(TOP_K):
            _row_copy(yb_hbm, dest_smem[t * TOP_K + k], rows_ref.at[k], t, sem_rows).start()
        return c

    lax.fori_loop(0, TM_MOE, issue, 0)

    def drain(t, c):
        for k in range(TOP_K):
            _row_copy(yb_hbm, 0, rows_ref.at[k], 0, sem_rows).wait()
        return c

    lax.fori_loop(0, TM_MOE, drain, 0)

    gate = gate_ref[...]
    ff = gate[:, 0:1] * rows_ref[0]
    for k in range(1, TOP_K):
        ff = ff + gate[:, k:k + 1] * rows_ref[k]

    @pl.when(i < n_prompt)
    def _():
        yp_ref[...] = _layer_norm(DEEPNORM_ALPHA * hp_ref[...] + ff, g_ref[...], b_ref[...])

    @pl.when(i >= n_prompt)
    def _():
        ys_ref[...] = _layer_norm(DEEPNORM_ALPHA * hs_ref[...] + ff, g_ref[...], b_ref[...])


def _combine(h2_p, h2_s, gates, dest_flat, yb, ln_g, ln_b):
    n_prompt = h2_p.shape[0] // TM_MOE
    n_sample = h2_s.shape[0] // TM_MOE
    pmap = lambda i: (jnp.minimum(i, n_prompt - 1), 0)
    smap = lambda i: (jnp.maximum(i - n_prompt, 0), 0)
    vec = pl.BlockSpec((1, D_MODEL), lambda i: (0, 0))
    return pl.pallas_call(
        functools.partial(_combine_kernel, n_prompt=n_prompt),
        grid=(n_prompt + n_sample,),
        in_specs=[pl.BlockSpec((TM_MOE, D_MODEL), pmap), pl.BlockSpec((TM_MOE, D_MODEL), smap),
                  pl.BlockSpec((TM_MOE, TOP_K), lambda i: (i, 0)),
                  pl.BlockSpec(memory_space=pl.ANY), pl.BlockSpec(memory_space=pl.ANY), vec, vec],
        out_specs=[pl.BlockSpec((TM_MOE, D_MODEL), pmap), pl.BlockSpec((TM_MOE, D_MODEL), smap)],
        out_shape=[jax.ShapeDtypeStruct(h2_p.shape, F32), jax.ShapeDtypeStruct(h2_s.shape, F32)],
        scratch_shapes=[pltpu.SMEM((TM_MOE * TOP_K,), I32), pltpu.VMEM((TOP_K, TM_MOE, D_MODEL), F32),
                        pltpu.SemaphoreType.DMA, pltpu.SemaphoreType.DMA],
        compiler_params=_cparams("arbitrary"),
        name="moe_combine",
    )(h2_p, h2_s, gates, dest_flat, yb, ln_g, ln_b)


def _tokens_major(a):
    n, k, tm = a.shape
    return jnp.transpose(a, (1, 0, 2)).reshape(k, n * tm)


def kernel(x_prompt, x_sample, mem_prompt, cache_diff_k, cache_diff_v, state_pool, cache_mem_k, cache_mem_v, w_in, lambda_q1, lambda_k1, lambda_q2, lambda_k2, subln_g, pool_w, pool_scale, w_out, ln1_g, ln1_b, xa_wq, xa_wk, xa_wv, xa_wo, ln2_g, ln2_b, router_w, router_b, moe_w_gate, moe_b_gate, moe_w_up, moe_b_up, moe_w_down, moe_b_down, ln3_g, ln3_b):
    assert w_in.shape[0] == DEPTH == 1
    batch, seq, d = x_prompt.shape
    dec_batch, dec_seq, _ = x_sample.shape
    past_len = cache_diff_k.shape[2]
    lambda_init = 0.8 - 0.6 * math.exp(-0.3 * 0)
    tp, ts = batch * seq, dec_batch * dec_seq

    vec = lambda a: a[0].reshape(1, -1)
    rw_t = router_w[0].T
    rw_hi = rw_t.astype(BF16)
    wts = dict(
        pool_w=pool_w[0].astype(BF16), pool_scale=vec(pool_scale), w_out=w_out[0].astype(BF16),
        ln1_g=vec(ln1_g), ln1_b=vec(ln1_b), xa_wq=xa_wq[0].astype(BF16), xa_wo=xa_wo[0].astype(BF16),
        ln2_g=vec(ln2_g), ln2_b=vec(ln2_b), rw_hi=rw_hi, rw_lo=(rw_t - rw_hi.astype(F32)).astype(BF16),
        router_b=router_b[0].reshape(N_EXPERTS, 1))
    w_in_bf = w_in[0].astype(BF16)
    lam_vecs = [vec(lambda_q1), vec(lambda_k1), vec(lambda_q2), vec(lambda_k2)]
    g_sub = vec(subln_g)

    xp2 = x_prompt.reshape(tp, d)
    qp, kp, vp, up, kpb, vpb = _inproj(xp2, w_in_bf, TM_PROJ)
    dp = _diff_prompt(qp, kpb, vpb, lam_vecs, g_sub, batch, seq, lambda_init)
    mk, mv, mkb, mvb = _memkv(mem_prompt.reshape(batch * N_MEM, d), xa_wk[0].astype(BF16),
                              xa_wv[0].astype(BF16), TM_PROJ)
    per16 = TM_POST // HIST_ROWS
    h2_p, h2p_p, idx_p, gate_p = _post(
        dp, up, up, lambda i: (jnp.maximum(i * per16 - 1, 0), 0), xp2, mkb, mvb, wts,
        tm=TM_POST, seq=seq, pos0=0, zero_first_hist=True)

    xs2 = x_sample.reshape(ts, d)
    qs, ks, vs, us, ksb, vsb = _inproj(xs2, w_in_bf, TM_PROJ)
    ds = _diff_sample(qs, cache_diff_k[0].reshape(dec_batch, past_len, DIFF_WIDTH),
                      cache_diff_v[0].reshape(dec_batch, past_len, DIFF_WIDTH), ksb, vsb, lam_vecs, g_sub,
                      dec_batch, dec_seq, lambda_init)
    hist_s = jnp.pad(state_pool[0], ((0, 0), (1, 0), (0, 0))).reshape(dec_batch * HIST_ROWS, POOL_WIDTH)
    h2_s, h2p_s, idx_s, gate_s = _post(
        ds, us, hist_s, lambda i: (i, 0), xs2,
        cache_mem_k[0].reshape(dec_batch * N_MEM, d).astype(BF16),
        cache_mem_v[0].reshape(dec_batch * N_MEM, d).astype(BF16), wts,
        tm=dec_seq, seq=dec_seq, pos0=past_len, zero_first_hist=False)

    t_all = tp + ts
    idx_all = jnp.concatenate([_tokens_major(idx_p), _tokens_major(idx_s)], axis=1)
    gate_all = jnp.concatenate([_tokens_major(gate_p), _tokens_major(gate_s)], axis=1)
    rank_all, counts = _ranks(idx_all)
    counts = counts[:, 0]
    padded = (counts + MOE_BLOCK - 1) // MOE_BLOCK * MOE_BLOCK
    pad_end = jnp.cumsum(padded)
    pad_start = (pad_end - padded).astype(I32)
    n_blocks = t_all * TOP_K // MOE_BLOCK + N_EXPERTS
    block_start = jnp.arange(n_blocks, dtype=I32) * MOE_BLOCK
    block_expert = jnp.minimum(jnp.searchsorted(pad_end, block_start, side='right'), N_EXPERTS - 1).astype(I32)
    block_active = (block_start < pad_end[-1]).astype(I32)
    dest_flat = _dests(pad_start, idx_all, rank_all).T.reshape(-1)
    xs = _dispatch(h2p_p, h2p_s, dest_flat, n_blocks * MOE_BLOCK)
    yb = _experts(block_expert, block_active, xs,
                  moe_w_gate[0].astype(BF16), moe_b_gate[0].reshape(N_EXPERTS, 1, D_MODEL),
                  moe_w_up[0].astype(BF16), moe_b_up[0].reshape(N_EXPERTS, 1, D_MODEL),
                  moe_w_down[0].astype(BF16), moe_b_down[0].reshape(N_EXPERTS, 1, D_MODEL))
    y_p, y_s = _combine(h2_p, h2_s, gate_all.T, dest_flat, yb, vec(ln3_g), vec(ln3_b))

    heads = (N_DIFF_HEADS, DIFF_V_DIM)
    xa = (N_XA_HEADS, XA_HEAD_DIM)
    up3 = up.reshape(batch, seq, POOL_WIDTH)
    us3 = us.reshape(dec_batch, dec_seq, POOL_WIDTH)
    pool_s = jnp.concatenate([state_pool[0].astype(F32), us3], axis=1)[:, -POOL_HIST:]
    return (y_p.reshape(batch, seq, d), y_s.reshape(dec_batch, dec_seq, d),
            kp.reshape(1, batch, seq, *heads), vp.reshape(1, batch, seq, *heads),
            up3[:, seq - POOL_HIST:][None],
            mk.reshape(1, batch, N_MEM, *xa), mv.reshape(1, batch, N_MEM, *xa),
            ks.reshape(1, dec_batch, dec_seq, *heads), vs.reshape(1, dec_batch, dec_seq, *heads),
            pool_s[None])
```

```python
import functools
import math

import jax
import jax.numpy as jnp
from jax import lax
from jax.experimental import pallas as pl
from jax.experimental.pallas import tpu as pltpu

F32 = jnp.float32
BF16 = jnp.bfloat16
I32 = jnp.int32
U32 = jnp.uint32

D_MODEL = 1024
CHUNK = 64
N_DIFF_HEADS = 4
DIFF_HEAD_DIM = 64
DIFF_V_DIM = 2 * DIFF_HEAD_DIM
DIFF_WIDTH = N_DIFF_HEADS * DIFF_V_DIM
POOL_WIDTH = D_MODEL - DIFF_WIDTH
POOL_WINDOWS = (2, 4, 8, 16)
POOL_GROUP_DIM = POOL_WIDTH // len(POOL_WINDOWS)
POOL_HIST = max(POOL_WINDOWS) - 1
HIST_ROWS = POOL_HIST + 1
IN_WIDTH = 3 * DIFF_WIDTH + POOL_WIDTH
N_MEM = 256
N_XA_HEADS = 4
XA_HEAD_DIM = D_MODEL // N_XA_HEADS
N_EXPERTS = 32
TOP_K = 4
SWIGLU_LIMIT = 7.0
SWIGLU_ALPHA = 1.702
MOE_BLOCK = 256
LN_EPS = 1e-5
RMS_EPS = 1e-5
DEPTH = 1
DEEPNORM_ALPHA = (2.0 * DEPTH) ** 0.25
NEG_BIG = -1e30

VMEM_LIMIT = 56 * 1024 * 1024

TM_PROJ = 512
TQ_ATT = 512
TK_ATT = 512
TK_PAST = 1024
TM_POST = 512
TR_RANK = 512
TM_MOE = 256


def _cparams(*sem):
    return pltpu.CompilerParams(dimension_semantics=sem, vmem_limit_bytes=VMEM_LIMIT)


def _layer_norm(x, g, b):
    mu = jnp.mean(x, axis=-1, keepdims=True)
    xc = x - mu
    var = jnp.mean(xc * xc, axis=-1, keepdims=True)
    return xc * lax.rsqrt(var + LN_EPS) * g + b


def _inproj_kernel(x_ref, w_ref, q_ref, k_ref, v_ref, u_ref, kb_ref, vb_ref):
    x = x_ref[...].astype(BF16)
    w = DIFF_WIDTH
    q = jnp.dot(x, w_ref[:, 0:w], preferred_element_type=F32)
    q_ref[...] = (q * (DIFF_HEAD_DIM ** -0.5)).astype(BF16)
    k = jnp.dot(x, w_ref[:, w:2 * w], preferred_element_type=F32)
    kb_ref[...] = k.astype(BF16)
    v = jnp.dot(x, w_ref[:, 2 * w:3 * w], preferred_element_type=F32)
    vb_ref[...] = v.astype(BF16)
    for h in range(N_DIFF_HEADS):
        cs = slice(h * DIFF_V_DIM, (h + 1) * DIFF_V_DIM)
        k_ref[:, h, :] = k[:, cs]
        v_ref[:, h, :] = v[:, cs]
    u_ref[...] = jnp.dot(x, w_ref[:, 3 * w:], preferred_element_type=F32)


def _inproj(x2d, w_in_bf, tm):
    t = x2d.shape[0]
    w = DIFF_WIDTH
    row = lambda i: (i, 0)
    out_spec = pl.BlockSpec((tm, w), row)
    head_spec = pl.BlockSpec((tm, N_DIFF_HEADS, DIFF_V_DIM), lambda i: (i, 0, 0))
    head_shape = jax.ShapeDtypeStruct((t, N_DIFF_HEADS, DIFF_V_DIM), F32)
    return pl.pallas_call(
        _inproj_kernel,
        grid=(t // tm,),
        in_specs=[pl.BlockSpec((tm, D_MODEL), row),
                  pl.BlockSpec((D_MODEL, IN_WIDTH), lambda i: (0, 0))],
        out_specs=[out_spec, head_spec, head_spec, out_spec, out_spec, out_spec],
        out_shape=[jax.ShapeDtypeStruct((t, w), BF16), head_shape, head_shape,
                   jax.ShapeDtypeStruct((t, w), F32),
                   jax.ShapeDtypeStruct((t, w), BF16), jax.ShapeDtypeStruct((t, w), BF16)],
        compiler_params=_cparams("parallel"),
        name="inproj",
    )(x2d, w_in_bf)


def _memkv_kernel(x_ref, wk_ref, wv_ref, k_ref, v_ref, kb_ref, vb_ref):
    x = x_ref[...].astype(BF16)
    k = jnp.dot(x, wk_ref[...], preferred_element_type=F32)
    kb_ref[...] = k.astype(BF16)
    v = jnp.dot(x, wv_ref[...], preferred_element_type=F32)
    vb_ref[...] = v.astype(BF16)
    for h in range(N_XA_HEADS):
        cs = slice(h * XA_HEAD_DIM, (h + 1) * XA_HEAD_DIM)
        k_ref[:, h, :] = k[:, cs]
        v_ref[:, h, :] = v[:, cs]


def _memkv(mem2d, wk_bf, wv_bf, tm):
    t = mem2d.shape[0]
    row = lambda i: (i, 0)
    full = lambda i: (0, 0)
    spec = pl.BlockSpec((tm, D_MODEL), row)
    head_spec = pl.BlockSpec((tm, N_XA_HEADS, XA_HEAD_DIM), lambda i: (i, 0, 0))
    head_shape = jax.ShapeDtypeStruct((t, N_XA_HEADS, XA_HEAD_DIM), F32)
    return pl.pallas_call(
        _memkv_kernel,
        grid=(t // tm,),
        in_specs=[spec, pl.BlockSpec((D_MODEL, D_MODEL), full), pl.BlockSpec((D_MODEL, D_MODEL), full)],
        out_specs=[head_spec, head_spec, spec, spec],
        out_shape=[head_shape, head_shape,
                   jax.ShapeDtypeStruct((t, D_MODEL), BF16), jax.ShapeDtypeStruct((t, D_MODEL), BF16)],
        compiler_params=_cparams("parallel"),
        name="memkv",
    )(mem2d, wk_bf, wv_bf)


def _diff_lambda(lq1_ref, lk1_ref, lq2_ref, lk2_ref, lambda_init):
    s1 = jnp.sum(lq1_ref[...] * lk1_ref[...], axis=-1, keepdims=True)
    s2 = jnp.sum(lq2_ref[...] * lk2_ref[...], axis=-1, keepdims=True)
    return jnp.exp(s1) - jnp.exp(s2) + lambda_init


def _stack_maps(qh):
    lane = lax.broadcasted_iota(I32, qh.shape, 1)
    zero = jnp.zeros_like(qh)
    return jnp.concatenate([jnp.where(lane < DIFF_HEAD_DIM, qh, zero),
                            jnp.where(lane >= DIFF_HEAD_DIM, qh, zero)], axis=0)


def _softmax_step(qq, kh, vh, m_ref, acc_ref, mask):
    s = lax.dot_general(qq, kh, (((1,), (1,)), ((), ())), preferred_element_type=F32)
    if mask is not None:
        s = jnp.where(mask, s, NEG_BIG)
    m_old = m_ref[...]
    m_new = jnp.maximum(m_old, jnp.max(s, axis=-1, keepdims=True))
    p = jnp.exp(s - m_new).astype(BF16)
    ones_col = (lax.broadcasted_iota(I32, vh.shape, 1) == 0).astype(BF16)
    pv = jnp.dot(p, jnp.concatenate([vh, ones_col], axis=1), preferred_element_type=F32)
    acc_ref[...] = jnp.exp(m_old - m_new) * acc_ref[...] + pv
    m_ref[...] = m_new


def _diff_finish(acc, lam, g, lambda_init, tq):
    hd = DIFF_V_DIM
    o = acc[:tq, :hd] / acc[:tq, hd:hd + 1] - lam * (acc[tq:, :hd] / acc[tq:, hd:hd + 1])
    o = o * lax.rsqrt(jnp.mean(o * o, axis=-1, keepdims=True) + RMS_EPS)
    return o * g * (1.0 - lambda_init)


def _diff_prompt_kernel(q_ref, k_ref, v_ref, lq1_ref, lk1_ref, lq2_ref, lk2_ref, g_ref, o_ref, m_ref, acc_ref,
                        *, lambda_init):
    i = pl.program_id(1)
    tq, tk, hd = TQ_ATT, TK_ATT, DIFF_V_DIM
    lam = _diff_lambda(lq1_ref, lk1_ref, lq2_ref, lk2_ref, lambda_init)
    g = g_ref[...]
    n_full = (i * tq) // tk
    qrow = lax.broadcasted_iota(I32, (2 * tq, tk), 0)
    qrow = jnp.where(qrow >= tq, qrow - tq, qrow) + i * tq
    kcol = lax.broadcasted_iota(I32, (2 * tq, tk), 1) + n_full * tk
    last_mask = (kcol // CHUNK) <= (qrow // CHUNK)
    for h in range(N_DIFF_HEADS):
        cs = slice(h * hd, (h + 1) * hd)
        qq = _stack_maps(q_ref[:, cs])
        m_ref[...] = jnp.full(m_ref.shape, NEG_BIG, F32)
        acc_ref[...] = jnp.zeros(acc_ref.shape, F32)

        def step(j, mask):
            r0 = pl.multiple_of(j * tk, tk)
            _softmax_step(qq, k_ref[pl.ds(r0, tk), cs], v_ref[pl.ds(r0, tk), cs], m_ref, acc_ref, mask)

        def body(j, c):
            step(j, None)
            return c

        lax.fori_loop(0, n_full, body, 0)
        step(n_full, last_mask)
        o_ref[:, cs] = _diff_finish(acc_ref[...], lam, g, lambda_init, tq).astype(o_ref.dtype)


def _diff_prompt(qb, kb, vb, lam_vecs, subln_g, batch, seq, lambda_init):
    assert TQ_ATT % CHUNK == 0 and TK_ATT % TQ_ATT == 0 and seq % TK_ATT == 0
    nq = seq // TQ_ATT
    w = DIFF_WIDTH
    small = lambda b, i: (0, 0)
    return pl.pallas_call(
        functools.partial(_diff_prompt_kernel, lambda_init=lambda_init),
        grid=(batch, nq),
        in_specs=[pl.BlockSpec((TQ_ATT, w), lambda b, i: (b * nq + i, 0)),
                  pl.BlockSpec((seq, w), lambda b, i: (b, 0)),
                  pl.BlockSpec((seq, w), lambda b, i: (b, 0))]
                 + [pl.BlockSpec((1, DIFF_HEAD_DIM), small)] * 4
                 + [pl.BlockSpec((1, DIFF_V_DIM), small)],
        out_specs=pl.BlockSpec((TQ_ATT, w), lambda b, i: (b * nq + i, 0)),
        out_shape=jax.ShapeDtypeStruct((batch * seq, w), BF16),
        scratch_shapes=[pltpu.VMEM((2 * TQ_ATT, 1), F32), pltpu.VMEM((2 * TQ_ATT, 2 * DIFF_V_DIM), F32)],
        compiler_params=_cparams("parallel", "arbitrary"),
        name="diff_attn_prompt",
    )(qb, kb, vb, *lam_vecs, subln_g)


def _diff_sample_kernel(q_ref, pk_ref, pv_ref, nk_ref, nv_ref, lq1_ref, lk1_ref, lq2_ref, lk2_ref, g_ref,
                        o_ref, m_ref, acc_ref, *, lambda_init, n_past):
    j = pl.program_id(1)
    tq, hd = q_ref.shape[0], DIFF_V_DIM

    @pl.when(j == 0)
    def _():
        m_ref[...] = jnp.full(m_ref.shape, NEG_BIG, F32)
        acc_ref[...] = jnp.zeros(acc_ref.shape, F32)

    def update(k_of, v_of):
        for h in range(N_DIFF_HEADS):
            qq = _stack_maps(q_ref[:, h * hd:(h + 1) * hd])
            _softmax_step(qq, k_of(h), v_of(h), m_ref.at[h], acc_ref.at[h], None)

    @pl.when(j < n_past)
    def _():
        update(lambda h: pk_ref[0, 0, :, h, :].astype(BF16), lambda h: pv_ref[0, 0, :, h, :].astype(BF16))

    @pl.when(j == n_past)
    def _():
        update(lambda h: nk_ref[:, h * hd:(h + 1) * hd], lambda h: nv_ref[:, h * hd:(h + 1) * hd])
        lam = _diff_lambda(lq1_ref, lk1_ref, lq2_ref, lk2_ref, lambda_init)
        for h in range(N_DIFF_HEADS):
            o_ref[:, h * hd:(h + 1) * hd] = _diff_finish(acc_ref[h], lam, g_ref[...], lambda_init,
                                                         tq).astype(o_ref.dtype)


def _diff_sample(qb, past_k, past_v, kb, vb, lam_vecs, subln_g, batch, seq, lambda_init):
    past_len = past_k.shape[2]
    assert past_len % TK_PAST == 0
    n_past = past_len // TK_PAST
    w = DIFF_WIDTH
    small = lambda b, j: (0, 0)
    past_spec = pl.BlockSpec((1, 1, TK_PAST, N_DIFF_HEADS, DIFF_V_DIM),
                             lambda b, j: (0, b, jnp.minimum(j, n_past - 1), 0, 0))
    new_spec = pl.BlockSpec((seq, w), lambda b, j: (b, 0))
    return pl.pallas_call(
        functools.partial(_diff_sample_kernel, lambda_init=lambda_init, n_past=n_past),
        grid=(batch, n_past + 1),
        in_specs=[new_spec, past_spec, past_spec, new_spec, new_spec]
                 + [pl.BlockSpec((1, DIFF_HEAD_DIM), small)] * 4
                 + [pl.BlockSpec((1, DIFF_V_DIM), small)],
        out_specs=new_spec,
        out_shape=jax.ShapeDtypeStruct((batch * seq, w), BF16),
        scratch_shapes=[pltpu.VMEM((N_DIFF_HEADS, 2 * seq, 1), F32),
                        pltpu.VMEM((N_DIFF_HEADS, 2 * seq, 2 * DIFF_V_DIM), F32)],
        compiler_params=_cparams("parallel", "arbitrary"),
        name="diff_attn_sample",
    )(qb, past_k, past_v, kb, vb, *lam_vecs, subln_g)


def _post_kernel(d_ref, u_ref, hist_ref, x_ref, pw_ref, ps_ref, wo1_ref, g1_ref, b1_ref, wq_ref, mk_ref,
                 mv_ref, wo2_ref, g2_ref, b2_ref, rwh_ref, rwl_ref, rb_ref,
                 h2_ref, h2p_ref, idx_ref, gate_ref, full_ref, *, tm, seq, pos0, zero_first_hist):
    i = pl.program_id(0)
    row0 = (i * tm) % seq

    hist = hist_ref[...]
    if zero_first_hist:
        hist = jnp.where(row0 == 0, jnp.zeros_like(hist), hist)
    full_ref[0:HIST_ROWS, :] = hist
    u = u_ref[...]
    full_ref[HIST_ROWS:, :] = u
    pos = pos0 + row0 + lax.broadcasted_iota(I32, (tm, 1), 0)
    pooled = []
    for gi, w in enumerate(POOL_WINDOWS):
        cs = slice(gi * POOL_GROUP_DIM, (gi + 1) * POOL_GROUP_DIM)
        acc = u[:, cs]
        for back in range(1, w):
            acc = acc + full_ref[HIST_ROWS - back:HIST_ROWS - back + tm, cs]
        cnt = jnp.minimum(w, pos + 1).astype(F32)
        m = acc / cnt - u[:, cs]
        y = jnp.dot(m.astype(BF16), pw_ref[gi], preferred_element_type=F32)
        pooled.append((y * ps_ref[:, cs]).astype(BF16))
    mixed_in = jnp.concatenate([d_ref[...]] + pooled, axis=-1)

    mix = jnp.dot(mixed_in, wo1_ref[...], preferred_element_type=F32)
    h1 = _layer_norm(DEEPNORM_ALPHA * x_ref[...] + mix, g1_ref[...], b1_ref[...])

    q = jnp.dot(h1.astype(BF16), wq_ref[...], preferred_element_type=F32)
    qb = (q * (XA_HEAD_DIM ** -0.5)).astype(BF16)
    heads = []
    for h in range(N_XA_HEADS):
        cs = slice(h * XA_HEAD_DIM, (h + 1) * XA_HEAD_DIM)
        s = lax.dot_general(qb[:, cs], mk_ref[:, cs], (((1,), (1,)), ((), ())), preferred_element_type=F32)
        e = jnp.exp(s - jnp.max(s, axis=-1, keepdims=True))
        p = e / jnp.sum(e, axis=-1, keepdims=True)
        heads.append(jnp.dot(p.astype(BF16), mv_ref[:, cs], preferred_element_type=F32).astype(BF16))
    ca = jnp.dot(jnp.concatenate(heads, axis=-1), wo2_ref[...], preferred_element_type=F32)
    h2 = _layer_norm(DEEPNORM_ALPHA * h1 + ca, g2_ref[...], b2_ref[...])
    h2_ref[...] = h2

    hb = h2.astype(BF16)
    bits = pltpu.bitcast(hb.astype(F32), U32)
    half = D_MODEL // 2
    h2p_ref[...] = (bits[:, :half] >> 16) | (bits[:, half:] & jnp.uint32(0xFFFF0000))

    hl = (h2 - hb.astype(F32)).astype(BF16)
    dn = (((1,), (1,)), ((), ()))
    logits = (lax.dot_general(rwh_ref[...], hb, dn, preferred_element_type=F32)
              + lax.dot_general(rwh_ref[...], hl, dn, preferred_element_type=F32)
              + lax.dot_general(rwl_ref[...], hb, dn, preferred_element_type=F32)
              + rb_ref[...])
    erow = lax.broadcasted_iota(I32, logits.shape, 0)
    vals, idxs = [], []
    for _ in range(TOP_K):
        mx = jnp.max(logits, axis=0, keepdims=True)
        ix = jnp.min(jnp.where(logits == mx, erow, N_EXPERTS), axis=0, keepdims=True)
        vals.append(mx)
        idxs.append(ix)
        logits = jnp.where(erow == ix, -jnp.inf, logits)
    ex = [jnp.exp(v - vals[0]) for v in vals]
    den = ex[0] + ex[1] + ex[2] + ex[3]
    idx_ref[0] = jnp.concatenate(idxs, axis=0)
    gate_ref[0] = jnp.concatenate([e / den for e in ex], axis=0)


def _post(diff_out, u, hist_arr, hist_map, x2d, mkb, mvb, wts, *, tm, seq, pos0, zero_first_hist):
    t = x2d.shape[0]
    n = t // tm
    assert seq % tm == 0 or tm % seq == 0
    row = lambda i: (i, 0)
    full2 = lambda i: (0, 0)
    mem_map = lambda i: ((i * tm) // seq, 0)
    vec = pl.BlockSpec((1, D_MODEL), full2)
    wspec = pl.BlockSpec((D_MODEL, D_MODEL), full2)
    in_specs = [
        pl.BlockSpec((tm, DIFF_WIDTH), row), pl.BlockSpec((tm, POOL_WIDTH), row),
        pl.BlockSpec((HIST_ROWS, POOL_WIDTH), hist_map), pl.BlockSpec((tm, D_MODEL), row),
        pl.BlockSpec((len(POOL_WINDOWS), POOL_GROUP_DIM, POOL_GROUP_DIM), lambda i: (0, 0, 0)),
        pl.BlockSpec((1, POOL_WIDTH), full2),
        wspec, vec, vec,
        wspec, pl.BlockSpec((N_MEM, D_MODEL), mem_map), pl.BlockSpec((N_MEM, D_MODEL), mem_map),
        wspec, vec, vec,
        pl.BlockSpec((N_EXPERTS, D_MODEL), full2), pl.BlockSpec((N_EXPERTS, D_MODEL), full2),
        pl.BlockSpec((N_EXPERTS, 1), full2),
    ]
    out_specs = [pl.BlockSpec((tm, D_MODEL), row), pl.BlockSpec((tm, D_MODEL // 2), row),
                 pl.BlockSpec((1, TOP_K, tm), lambda i: (i, 0, 0)),
                 pl.BlockSpec((1, TOP_K, tm), lambda i: (i, 0, 0))]
    out_shape = [jax.ShapeDtypeStruct((t, D_MODEL), F32), jax.ShapeDtypeStruct((t, D_MODEL // 2), U32),
                 jax.ShapeDtypeStruct((n, TOP_K, tm), I32), jax.ShapeDtypeStruct((n, TOP_K, tm), F32)]
    return pl.pallas_call(
        functools.partial(_post_kernel, tm=tm, seq=seq, pos0=pos0, zero_first_hist=zero_first_hist),
        grid=(n,),
        in_specs=in_specs,
        out_specs=out_specs,
        out_shape=out_shape,
        scratch_shapes=[pltpu.VMEM((HIST_ROWS + tm, POOL_WIDTH), F32)],
        compiler_params=_cparams("parallel"),
        name="post_attn",
    )(diff_out, u, hist_arr, x2d, wts["pool_w"], wts["pool_scale"], wts["w_out"], wts["ln1_g"], wts["ln1_b"],
      wts["xa_wq"], mkb, mvb, wts["xa_wo"], wts["ln2_g"], wts["ln2_b"], wts["rw_hi"], wts["rw_lo"],
      wts["router_b"])


def _rank_kernel(idx_ref, rank_ref, cnt_ref, carry_ref):
    @pl.when(pl.program_id(0) == 0)
    def _():
        carry_ref[...] = jnp.zeros(carry_ref.shape, F32)

    idx = idx_ref[...]
    tr = idx.shape[1]
    erow = lax.broadcasted_iota(I32, (N_EXPERTS, tr), 0)
    hits = [erow == idx[k:k + 1, :] for k in range(TOP_K)]
    onehot = sum(h.astype(F32) for h in hits)
    earlier = (lax.broadcasted_iota(I32, (tr, tr), 0) < lax.broadcasted_iota(I32, (tr, tr), 1)).astype(BF16)
    before = jnp.dot(onehot.astype(BF16), earlier, preferred_element_type=F32) + carry_ref[:, 0:1]
    ranks = [jnp.sum(jnp.where(h, before, 0.0), axis=0, keepdims=True) for h in hits]
    rank_ref[...] = jnp.concatenate(ranks, axis=0).astype(I32)
    carry_ref[...] = carry_ref[...] + jnp.sum(onehot, axis=1, keepdims=True)
    cnt_ref[...] = carry_ref[...].astype(I32)


def _ranks(idx_all):
    t = idx_all.shape[1]
    assert t % TR_RANK == 0
    return pl.pallas_call(
        _rank_kernel,
        grid=(t // TR_RANK,),
        in_specs=[pl.BlockSpec((TOP_K, TR_RANK), lambda i: (0, i))],
        out_specs=[pl.BlockSpec((TOP_K, TR_RANK), lambda i: (0, i)),
                   pl.BlockSpec((N_EXPERTS, 128), lambda i: (0, 0))],
        out_shape=[jax.ShapeDtypeStruct((TOP_K, t), I32), jax.ShapeDtypeStruct((N_EXPERTS, 128), I32)],
        scratch_shapes=[pltpu.VMEM((N_EXPERTS, 128), F32)],
        compiler_params=_cparams("arbitrary"),
        name="moe_rank",
    )(idx_all)


def _dest_kernel(start_ref, idx_ref, rank_ref, dest_ref):
    idx = idx_ref[...]
    base = jnp.zeros(idx.shape, I32)
    for e in range(N_EXPERTS):
        base = jnp.where(idx == e, start_ref[e], base)
    dest_ref[...] = base + rank_ref[...]


def _dests(pad_start, idx_all, rank_all):
    t = idx_all.shape[1]
    spec = pl.BlockSpec((TOP_K, TR_RANK), lambda i, s: (0, i))
    return pl.pallas_call(
        _dest_kernel,
        grid_spec=pltpu.PrefetchScalarGridSpec(
            num_scalar_prefetch=1, grid=(t // TR_RANK,), in_specs=[spec, spec], out_specs=spec),
        out_shape=jax.ShapeDtypeStruct((TOP_K, t), I32),
        compiler_params=_cparams("parallel"),
        name="moe_dest",
    )(pad_start, idx_all, rank_all)


def _row_copy(src_ref, src_row, dst_ref, dst_row, sem):
    return pltpu.make_async_copy(src_ref.at[pl.ds(src_row, 1)], dst_ref.at[pl.ds(dst_row, 1)], sem)


def _dispatch_kernel(cnt_ref, start_ref, hp_ref, hs_ref, dest_hbm, xs_out, dest_smem, zero_ref,
                     sem_idx, sem_rows, *, n_prompt):
    i = pl.program_id(0)
    n = TM_MOE * TOP_K
    cp = pltpu.make_async_copy(dest_hbm.at[pl.ds(i * n, n)], dest_smem, sem_idx)
    cp.start()

    @pl.when(i == 0)
    def _():
        zero_ref[...] = jnp.zeros(zero_ref.shape, U32)
        for e in range(N_EXPERTS):
            n_pad = (cnt_ref[e] + MOE_BLOCK - 1) // MOE_BLOCK * MOE_BLOCK - cnt_ref[e]
            first = start_ref[e] + cnt_ref[e]

            def fill(r, c):
                _row_copy(zero_ref, 0, xs_out, first + r, sem_rows).start()
                return c

            lax.fori_loop(0, n_pad, fill, 0)

            def drain_fill(r, c):
                _row_copy(zero_ref, 0, xs_out, 0, sem_rows).wait()
                return c

            lax.fori_loop(0, n_pad, drain_fill, 0)

        last = N_EXPERTS - 1
        n_used = (start_ref[last] + cnt_ref[last] + MOE_BLOCK - 1) // MOE_BLOCK
        n_blocks = xs_out.shape[0] // MOE_BLOCK

        def block_copy(b):
            return pltpu.make_async_copy(zero_ref, xs_out.at[pl.ds(b * MOE_BLOCK, MOE_BLOCK)], sem_rows)

        def fill_block(b, c):
            block_copy(b).start()
            return c

        lax.fori_loop(n_used, n_blocks, fill_block, 0)

        def drain_block(b, c):
            block_copy(b).wait()
            return c

        lax.fori_loop(n_used, n_blocks, drain_block, 0)

    cp.wait()

    def scatter_from(src_ref):
        def issue(t, c):
            for k in range(TOP_K):
                _row_copy(src_ref, t, xs_out, dest_smem[t * TOP_K + k], sem_rows).start()
            return c

        lax.fori_loop(0, TM_MOE, issue, 0)

        def drain(t, c):
            for k in range(TOP_K):
                _row_copy(src_ref, 0, xs_out, 0, sem_rows).wait()
            return c

        lax.fori_loop(0, TM_MOE, drain, 0)

    @pl.when(i < n_prompt)
    def _():
        scatter_from(hp_ref)

    @pl.when(i >= n_prompt)
    def _():
        scatter_from(hs_ref)


def _dispatch(counts, pad_start, h2p_p, h2p_s, dest_flat, cap):
    n_prompt = h2p_p.shape[0] // TM_MOE
    n_sample = h2p_s.shape[0] // TM_MOE
    half = D_MODEL // 2
    return pl.pallas_call(
        functools.partial(_dispatch_kernel, n_prompt=n_prompt),
        grid_spec=pltpu.PrefetchScalarGridSpec(
            num_scalar_prefetch=2, grid=(n_prompt + n_sample,),
            in_specs=[pl.BlockSpec((TM_MOE, half), lambda i, c, s: (jnp.minimum(i, n_prompt - 1), 0)),
                      pl.BlockSpec((TM_MOE, half), lambda i, c, s: (jnp.maximum(i - n_prompt, 0), 0)),
                      pl.BlockSpec(memory_space=pl.ANY)],
            out_specs=pl.BlockSpec(memory_space=pl.ANY),
            scratch_shapes=[pltpu.SMEM((TM_MOE * TOP_K,), I32), pltpu.VMEM((MOE_BLOCK, half), U32),
                            pltpu.SemaphoreType.DMA, pltpu.SemaphoreType.DMA]),
        out_shape=jax.ShapeDtypeStruct((cap, half), U32),
        compiler_params=_cparams("arbitrary"),
        name="moe_dispatch",
    )(counts, pad_start, h2p_p, h2p_s, dest_flat)


def _expert_kernel(be_ref, act_ref, new_ref, x_ref, wg_ref, bg_ref, wu_ref, bu_ref, wd_ref, bd_ref, o_ref,
                   wgb_ref, wub_ref, wdb_ref):
    b = pl.program_id(0)

    @pl.when(new_ref[b] == 1)
    def _():
        wgb_ref[...] = wg_ref[0].astype(BF16)
        wub_ref[...] = wu_ref[0].astype(BF16)
        wdb_ref[...] = wd_ref[0].astype(BF16)

    @pl.when(act_ref[b] == 1)
    def _():
        p = x_ref[...]
        lo = pltpu.bitcast(p << 16, F32)
        hi = pltpu.bitcast(p & jnp.uint32(0xFFFF0000), F32)
        x = jnp.concatenate([lo, hi], axis=-1).astype(BF16)
        g = jnp.dot(x, wgb_ref[...], preferred_element_type=F32) + bg_ref[0]
        u = jnp.dot(x, wub_ref[...], preferred_element_type=F32) + bu_ref[0]
        g = jnp.minimum(g, SWIGLU_LIMIT)
        u = jnp.clip(u, -SWIGLU_LIMIT, SWIGLU_LIMIT)
        a = (u + 1.0) * (g * jax.nn.sigmoid(SWIGLU_ALPHA * g))
        o_ref[...] = jnp.dot(a.astype(BF16), wdb_ref[...], preferred_element_type=F32) + bd_ref[0]

    @pl.when(act_ref[b] == 0)
    def _():
        o_ref[...] = jnp.zeros(o_ref.shape, F32)


def _experts(block_expert, block_active, block_new, xs, wg, bg, wu, bu, wd, bd):
    cap = xs.shape[0]
    n_blocks = cap // MOE_BLOCK
    wspec = pl.BlockSpec((1, D_MODEL, D_MODEL), lambda b, be, act, new: (be[b], 0, 0))
    bspec = pl.BlockSpec((1, 1, D_MODEL), lambda b, be, act, new: (be[b], 0, 0))
    return pl.pallas_call(
        _expert_kernel,
        grid_spec=pltpu.PrefetchScalarGridSpec(
            num_scalar_prefetch=3, grid=(n_blocks,),
            in_specs=[pl.BlockSpec((MOE_BLOCK, D_MODEL // 2), lambda b, be, act, new: (b * act[b], 0)),
                      wspec, bspec, wspec, bspec, wspec, bspec],
            out_specs=pl.BlockSpec((MOE_BLOCK, D_MODEL), lambda b, be, act, new: (b, 0)),
            scratch_shapes=[pltpu.VMEM((D_MODEL, D_MODEL), BF16)] * 3),
        out_shape=jax.ShapeDtypeStruct((cap, D_MODEL), F32),
        compiler_params=_cparams("arbitrary"),
        name="moe_experts",
    )(block_expert, block_active, block_new, xs, wg, bg, wu, bu, wd, bd)


def _combine_kernel(hp_ref, hs_ref, gate_ref, dest_hbm, yb_hbm, g_ref, b_ref, yp_ref, ys_ref,
                    dest_smem, rows_ref, sem_idx, sem_rows, *, n_prompt):
    i = pl.program_id(0)
    n = TM_MOE * TOP_K
    cp = pltpu.make_async_copy(dest_hbm.at[pl.ds(i * n, n)], dest_smem, sem_idx)
    cp.start()
    cp.wait()

    def issue(t, c):
        for k in range(TOP_K):
            _row_copy(yb_hbm, dest_smem[t * TOP_K + k], rows_ref.at[k], t, sem_rows).start()
        return c

    lax.fori_loop(0, TM_MOE, issue, 0)

    def drain(t, c):
        for k in range(TOP_K):
            _row_copy(yb_hbm, 0, rows_ref.at[k], 0, sem_rows).wait()
        return c

    lax.fori_loop(0, TM_MOE, drain, 0)

    gate = gate_ref[...]
    ff = gate[:, 0:1] * rows_ref[0]
    for k in range(1, TOP_K):
        ff = ff + gate[:, k:k + 1] * rows_ref[k]

    @pl.when(i < n_prompt)
    def _():
        yp_ref[...] = _layer_norm(DEEPNORM_ALPHA * hp_ref[...] + ff, g_ref[...], b_ref[...])

    @pl.when(i >= n_prompt)
    def _():
        ys_ref[...] = _layer_norm(DEEPNORM_ALPHA * hs_ref[...] + ff, g_ref[...], b_ref[...])


def _combine(h2_p, h2_s, gates, dest_flat, yb, ln_g, ln_b):
    n_prompt = h2_p.shape[0] // TM_MOE
    n_sample = h2_s.shape[0] // TM_MOE
    pmap = lambda i: (jnp.minimum(i, n_prompt - 1), 0)
    smap = lambda i: (jnp.maximum(i - n_prompt, 0), 0)
    vec = pl.BlockSpec((1, D_MODEL), lambda i: (0, 0))
    return pl.pallas_call(
        functools.partial(_combine_kernel, n_prompt=n_prompt),
        grid=(n_prompt + n_sample,),
        in_specs=[pl.BlockSpec((TM_MOE, D_MODEL), pmap), pl.BlockSpec((TM_MOE, D_MODEL), smap),
                  pl.BlockSpec((TM_MOE, TOP_K), lambda i: (i, 0)),
                  pl.BlockSpec(memory_space=pl.ANY), pl.BlockSpec(memory_space=pl.ANY), vec, vec],
        out_specs=[pl.BlockSpec((TM_MOE, D_MODEL), pmap), pl.BlockSpec((TM_MOE, D_MODEL), smap)],
        out_shape=[jax.ShapeDtypeStruct(h2_p.shape, F32), jax.ShapeDtypeStruct(h2_s.shape, F32)],
        scratch_shapes=[pltpu.SMEM((TM_MOE * TOP_K,), I32), pltpu.VMEM((TOP_K, TM_MOE, D_MODEL), F32),
                        pltpu.SemaphoreType.DMA, pltpu.SemaphoreType.DMA],
        compiler_params=_cparams("arbitrary"),
        name="moe_combine",
    )(h2_p, h2_s, gates, dest_flat, yb, ln_g, ln_b)


def _tokens_major(a):
    n, k, tm = a.shape
    return jnp.transpose(a, (1, 0, 2)).reshape(k, n * tm)


def kernel(x_prompt, x_sample, mem_prompt, cache_diff_k, cache_diff_v, state_pool, cache_mem_k, cache_mem_v, w_in, lambda_q1, lambda_k1, lambda_q2, lambda_k2, subln_g, pool_w, pool_scale, w_out, ln1_g, ln1_b, xa_wq, xa_wk, xa_wv, xa_wo, ln2_g, ln2_b, router_w, router_b, moe_w_gate, moe_b_gate, moe_w_up, moe_b_up, moe_w_down, moe_b_down, ln3_g, ln3_b):
    assert w_in.shape[0] == DEPTH == 1
    batch, seq, d = x_prompt.shape
    dec_batch, dec_seq, _ = x_sample.shape
    past_len = cache_diff_k.shape[2]
    lambda_init = 0.8 - 0.6 * math.exp(-0.3 * 0)
    tp, ts = batch * seq, dec_batch * dec_seq

    vec = lambda a: a[0].reshape(1, -1)
    rw_t = router_w[0].T
    rw_hi = rw_t.astype(BF16)
    wts = dict(
        pool_w=pool_w[0].astype(BF16), pool_scale=vec(pool_scale), w_out=w_out[0].astype(BF16),
        ln1_g=vec(ln1_g), ln1_b=vec(ln1_b), xa_wq=xa_wq[0].astype(BF16), xa_wo=xa_wo[0].astype(BF16),
        ln2_g=vec(ln2_g), ln2_b=vec(ln2_b), rw_hi=rw_hi, rw_lo=(rw_t - rw_hi.astype(F32)).astype(BF16),
        router_b=router_b[0].reshape(N_EXPERTS, 1))
    w_in_bf = w_in[0].astype(BF16)
    lam_vecs = [vec(lambda_q1), vec(lambda_k1), vec(lambda_q2), vec(lambda_k2)]
    g_sub = vec(subln_g)

    xp2 = x_prompt.reshape(tp, d)
    qp, kp, vp, up, kpb, vpb = _inproj(xp2, w_in_bf, TM_PROJ)
    dp = _diff_prompt(qp, kpb, vpb, lam_vecs, g_sub, batch, seq, lambda_init)
    mk, mv, mkb, mvb = _memkv(mem_prompt.reshape(batch * N_MEM, d), xa_wk[0].astype(BF16),
                              xa_wv[0].astype(BF16), TM_PROJ)
    per16 = TM_POST // HIST_ROWS
    h2_p, h2p_p, idx_p, gate_p = _post(
        dp, up, up, lambda i: (jnp.maximum(i * per16 - 1, 0), 0), xp2, mkb, mvb, wts,
        tm=TM_POST, seq=seq, pos0=0, zero_first_hist=True)

    xs2 = x_sample.reshape(ts, d)
    qs, ks, vs, us, ksb, vsb = _inproj(xs2, w_in_bf, TM_PROJ)
    ds = _diff_sample(qs, cache_diff_k, cache_diff_v, ksb, vsb, lam_vecs, g_sub, dec_batch, dec_seq, lambda_init)
    hist_s = jnp.pad(state_pool[0], ((0, 0), (1, 0), (0, 0))).reshape(dec_batch * HIST_ROWS, POOL_WIDTH)
    h2_s, h2p_s, idx_s, gate_s = _post(
        ds, us, hist_s, lambda i: (i, 0), xs2,
        cache_mem_k[0].reshape(dec_batch * N_MEM, d).astype(BF16),
        cache_mem_v[0].reshape(dec_batch * N_MEM, d).astype(BF16), wts,
        tm=dec_seq, seq=dec_seq, pos0=past_len, zero_first_hist=False)

    t_all = tp + ts
    idx_all = jnp.concatenate([_tokens_major(idx_p), _tokens_major(idx_s)], axis=1)
    gate_all = jnp.concatenate([_tokens_major(gate_p), _tokens_major(gate_s)], axis=1)
    rank_all, counts = _ranks(idx_all)
    counts = counts[:, 0]
    padded = (counts + MOE_BLOCK - 1) // MOE_BLOCK * MOE_BLOCK
    pad_end = jnp.cumsum(padded)
    pad_start = (pad_end - padded).astype(I32)
    n_blocks = t_all * TOP_K // MOE_BLOCK + N_EXPERTS
    block_start = jnp.arange(n_blocks, dtype=I32) * MOE_BLOCK
    block_expert = jnp.minimum(jnp.sum(block_start[:, None] >= pad_end[None, :], axis=1), N_EXPERTS - 1).astype(I32)
    block_active = (block_start < pad_end[-1]).astype(I32)
    block_new = jnp.concatenate([jnp.ones((1,), I32), (block_expert[1:] != block_expert[:-1]).astype(I32)])
    dest_flat = _dests(pad_start, idx_all, rank_all).T.reshape(-1)
    xs = _dispatch(counts, pad_start, h2p_p, h2p_s, dest_flat, n_blocks * MOE_BLOCK)
    yb = _experts(block_expert, block_active, block_new, xs,
                  moe_w_gate[0], moe_b_gate[0].reshape(N_EXPERTS, 1, D_MODEL),
                  moe_w_up[0], moe_b_up[0].reshape(N_EXPERTS, 1, D_MODEL),
                  moe_w_down[0], moe_b_down[0].reshape(N_EXPERTS, 1, D_MODEL))
    y_p, y_s = _combine(h2_p, h2_s, gate_all.T, dest_flat, yb, vec(ln3_g), vec(ln3_b))

    heads = (N_DIFF_HEADS, DIFF_V_DIM)
    xa = (N_XA_HEADS, XA_HEAD_DIM)
    up3 = up.reshape(batch, seq, POOL_WIDTH)
    us3 = us.reshape(dec_batch, dec_seq, POOL_WIDTH)
    pool_s = jnp.concatenate([state_pool[0].astype(F32), us3], axis=1)[:, -POOL_HIST:]
    return (y_p.reshape(batch, seq, d), y_s.reshape(dec_batch, dec_seq, d),
            kp.reshape(1, batch, seq, *heads), vp.reshape(1, batch, seq, *heads),
            up3[:, seq - POOL_HIST:][None],
            mk.reshape(1, batch, N_MEM, *xa), mv.reshape(1, batch, N_MEM, *xa),
            ks.reshape(1, dec_batch, dec_seq, *heads), vs.reshape(1, dec_batch, dec_seq, *heads),
            pool_s[None])
```

```python
import functools
import math

import jax
import jax.numpy as jnp
from jax import lax
from jax.experimental import pallas as pl
from jax.experimental.pallas import tpu as pltpu

F32 = jnp.float32
BF16 = jnp.bfloat16
I32 = jnp.int32
U32 = jnp.uint32

D_MODEL = 1024
CHUNK = 64
N_DIFF_HEADS = 4
DIFF_HEAD_DIM = 64
DIFF_V_DIM = 2 * DIFF_HEAD_DIM
DIFF_WIDTH = N_DIFF_HEADS * DIFF_V_DIM
POOL_WIDTH = D_MODEL - DIFF_WIDTH
POOL_WINDOWS = (2, 4, 8, 16)
POOL_GROUP_DIM = POOL_WIDTH // len(POOL_WINDOWS)
POOL_HIST = max(POOL_WINDOWS) - 1
HIST_ROWS = POOL_HIST + 1
IN_WIDTH = 3 * DIFF_WIDTH + POOL_WIDTH
N_MEM = 256
N_XA_HEADS = 4
XA_HEAD_DIM = D_MODEL // N_XA_HEADS
N_EXPERTS = 32
TOP_K = 4
SWIGLU_LIMIT = 7.0
SWIGLU_ALPHA = 1.702
MOE_BLOCK = 256
LN_EPS = 1e-5
RMS_EPS = 1e-5
DEPTH = 1
DEEPNORM_ALPHA = (2.0 * DEPTH) ** 0.25
NEG_BIG = -1e30

VMEM_LIMIT = 56 * 1024 * 1024

TM_PROJ = 512
TQ_ATT = 256
TK_ATT = 512
TK_PAST = 1024
TM_POST = 512
TR_RANK = 512
TM_MOE = 256


def _cparams(*sem):
    return pltpu.CompilerParams(dimension_semantics=sem, vmem_limit_bytes=VMEM_LIMIT)


def _layer_norm(x, g, b):
    mu = jnp.mean(x, axis=-1, keepdims=True)
    xc = x - mu
    var = jnp.mean(xc * xc, axis=-1, keepdims=True)
    return xc * lax.rsqrt(var + LN_EPS) * g + b


def _inproj_kernel(x_ref, w_ref, q_ref, k_ref, v_ref, u_ref, kb_ref, vb_ref):
    x = x_ref[...].astype(BF16)
    w = DIFF_WIDTH
    q = jnp.dot(x, w_ref[:, 0:w], preferred_element_type=F32)
    q_ref[...] = (q * (DIFF_HEAD_DIM ** -0.5)).astype(BF16)
    k = jnp.dot(x, w_ref[:, w:2 * w], preferred_element_type=F32)
    kb_ref[...] = k.astype(BF16)
    v = jnp.dot(x, w_ref[:, 2 * w:3 * w], preferred_element_type=F32)
    vb_ref[...] = v.astype(BF16)
    for h in range(N_DIFF_HEADS):
        cs = slice(h * DIFF_V_DIM, (h + 1) * DIFF_V_DIM)
        k_ref[:, h, :] = k[:, cs]
        v_ref[:, h, :] = v[:, cs]
    u_ref[...] = jnp.dot(x, w_ref[:, 3 * w:], preferred_element_type=F32)


def _inproj(x2d, w_in_bf, tm):
    t = x2d.shape[0]
    w = DIFF_WIDTH
    row = lambda i: (i, 0)
    out_spec = pl.BlockSpec((tm, w), row)
    head_spec = pl.BlockSpec((tm, N_DIFF_HEADS, DIFF_V_DIM), lambda i: (i, 0, 0))
    head_shape = jax.ShapeDtypeStruct((t, N_DIFF_HEADS, DIFF_V_DIM), F32)
    return pl.pallas_call(
        _inproj_kernel,
        grid=(t // tm,),
        in_specs=[pl.BlockSpec((tm, D_MODEL), row),
                  pl.BlockSpec((D_MODEL, IN_WIDTH), lambda i: (0, 0))],
        out_specs=[out_spec, head_spec, head_spec, out_spec, out_spec, out_spec],
        out_shape=[jax.ShapeDtypeStruct((t, w), BF16), head_shape, head_shape,
                   jax.ShapeDtypeStruct((t, w), F32),
                   jax.ShapeDtypeStruct((t, w), BF16), jax.ShapeDtypeStruct((t, w), BF16)],
        compiler_params=_cparams("parallel"),
        name="inproj",
    )(x2d, w_in_bf)


def _memkv_kernel(x_ref, wk_ref, wv_ref, k_ref, v_ref, kb_ref, vb_ref):
    x = x_ref[...].astype(BF16)
    k = jnp.dot(x, wk_ref[...], preferred_element_type=F32)
    kb_ref[...] = k.astype(BF16)
    v = jnp.dot(x, wv_ref[...], preferred_element_type=F32)
    vb_ref[...] = v.astype(BF16)
    for h in range(N_XA_HEADS):
        cs = slice(h * XA_HEAD_DIM, (h + 1) * XA_HEAD_DIM)
        k_ref[:, h, :] = k[:, cs]
        v_ref[:, h, :] = v[:, cs]


def _memkv(mem2d, wk_bf, wv_bf, tm):
    t = mem2d.shape[0]
    row = lambda i: (i, 0)
    full = lambda i: (0, 0)
    spec = pl.BlockSpec((tm, D_MODEL), row)
    head_spec = pl.BlockSpec((tm, N_XA_HEADS, XA_HEAD_DIM), lambda i: (i, 0, 0))
    head_shape = jax.ShapeDtypeStruct((t, N_XA_HEADS, XA_HEAD_DIM), F32)
    return pl.pallas_call(
        _memkv_kernel,
        grid=(t // tm,),
        in_specs=[spec, pl.BlockSpec((D_MODEL, D_MODEL), full), pl.BlockSpec((D_MODEL, D_MODEL), full)],
        out_specs=[head_spec, head_spec, spec, spec],
        out_shape=[head_shape, head_shape,
                   jax.ShapeDtypeStruct((t, D_MODEL), BF16), jax.ShapeDtypeStruct((t, D_MODEL), BF16)],
        compiler_params=_cparams("parallel"),
        name="memkv",
    )(mem2d, wk_bf, wv_bf)


def _diff_lambda(lq1_ref, lk1_ref, lq2_ref, lk2_ref, lambda_init):
    s1 = jnp.sum(lq1_ref[...] * lk1_ref[...], axis=-1, keepdims=True)
    s2 = jnp.sum(lq2_ref[...] * lk2_ref[...], axis=-1, keepdims=True)
    return jnp.exp(s1) - jnp.exp(s2) + lambda_init


def _stack_maps(qh):
    lane = lax.broadcasted_iota(I32, qh.shape, 1)
    zero = jnp.zeros_like(qh)
    return jnp.concatenate([jnp.where(lane < DIFF_HEAD_DIM, qh, zero),
                            jnp.where(lane >= DIFF_HEAD_DIM, qh, zero)], axis=0)


def _softmax_step(qq, kh, vh, m_ref, acc_ref, mask):
    s = lax.dot_general(qq, kh, (((1,), (1,)), ((), ())), preferred_element_type=F32)
    if mask is not None:
        s = jnp.where(mask, s, NEG_BIG)
    m_old = m_ref[...]
    m_new = jnp.maximum(m_old, jnp.max(s, axis=-1, keepdims=True))
    p = jnp.exp(s - m_new).astype(BF16)
    ones_col = (lax.broadcasted_iota(I32, vh.shape, 1) == 0).astype(BF16)
    pv = jnp.dot(p, jnp.concatenate([vh, ones_col], axis=1), preferred_element_type=F32)
    acc_ref[...] = jnp.exp(m_old - m_new) * acc_ref[...] + pv
    m_ref[...] = m_new


def _diff_finish(acc, lam, g, lambda_init, tq):
    hd = DIFF_V_DIM
    o = acc[:tq, :hd] / acc[:tq, hd:hd + 1] - lam * (acc[tq:, :hd] / acc[tq:, hd:hd + 1])
    o = o * lax.rsqrt(jnp.mean(o * o, axis=-1, keepdims=True) + RMS_EPS)
    return o * g * (1.0 - lambda_init)


def _diff_prompt_kernel(q_ref, k_ref, v_ref, lq1_ref, lk1_ref, lq2_ref, lk2_ref, g_ref, o_ref, m_ref, acc_ref,
                        *, lambda_init):
    i = pl.program_id(1)
    tq, tk, hd = TQ_ATT, TK_ATT, DIFF_V_DIM
    n_full = (i * tq) // tk
    qrow = lax.broadcasted_iota(I32, (2 * tq, tk), 0)
    qrow = jnp.where(qrow >= tq, qrow - tq, qrow) + i * tq
    kcol = lax.broadcasted_iota(I32, (2 * tq, tk), 1) + n_full * tk
    last_mask = (kcol // CHUNK) <= (qrow // CHUNK)
    m_ref[...] = jnp.full(m_ref.shape, NEG_BIG, F32)
    acc_ref[...] = jnp.zeros(acc_ref.shape, F32)

    def step(j, mask):
        r0 = pl.multiple_of(j * tk, tk)
        for h in range(N_DIFF_HEADS):
            cs = slice(h * hd, (h + 1) * hd)
            _softmax_step(_stack_maps(q_ref[:, cs]), k_ref[pl.ds(r0, tk), cs], v_ref[pl.ds(r0, tk), cs],
                          m_ref.at[h], acc_ref.at[h], mask)

    def body(j, c):
        step(j, None)
        return c

    lax.fori_loop(0, n_full, body, 0)
    step(n_full, last_mask)
    lam = _diff_lambda(lq1_ref, lk1_ref, lq2_ref, lk2_ref, lambda_init)
    for h in range(N_DIFF_HEADS):
        o_ref[:, h * hd:(h + 1) * hd] = _diff_finish(acc_ref[h], lam, g_ref[...], lambda_init,
                                                     tq).astype(o_ref.dtype)


def _diff_prompt(qb, kb, vb, lam_vecs, subln_g, batch, seq, lambda_init):
    assert TQ_ATT % CHUNK == 0 and TK_ATT % TQ_ATT == 0 and seq % TK_ATT == 0
    nq = seq // TQ_ATT
    w = DIFF_WIDTH
    small = lambda b, i: (0, 0)
    return pl.pallas_call(
        functools.partial(_diff_prompt_kernel, lambda_init=lambda_init),
        grid=(batch, nq),
        in_specs=[pl.BlockSpec((TQ_ATT, w), lambda b, i: (b * nq + i, 0)),
                  pl.BlockSpec((seq, w), lambda b, i: (b, 0)),
                  pl.BlockSpec((seq, w), lambda b, i: (b, 0))]
                 + [pl.BlockSpec((1, DIFF_HEAD_DIM), small)] * 4
                 + [pl.BlockSpec((1, DIFF_V_DIM), small)],
        out_specs=pl.BlockSpec((TQ_ATT, w), lambda b, i: (b * nq + i, 0)),
        out_shape=jax.ShapeDtypeStruct((batch * seq, w), BF16),
        scratch_shapes=[pltpu.VMEM((N_DIFF_HEADS, 2 * TQ_ATT, 1), F32),
                        pltpu.VMEM((N_DIFF_HEADS, 2 * TQ_ATT, 2 * DIFF_V_DIM), F32)],
        compiler_params=_cparams("parallel", "arbitrary"),
        name="diff_attn_prompt",
    )(qb, kb, vb, *lam_vecs, subln_g)


def _diff_sample_kernel(q_ref, pk_ref, pv_ref, nk_ref, nv_ref, lq1_ref, lk1_ref, lq2_ref, lk2_ref, g_ref,
                        o_ref, m_ref, acc_ref, *, lambda_init, n_past):
    j = pl.program_id(1)
    tq, hd = q_ref.shape[0], DIFF_V_DIM

    @pl.when(j == 0)
    def _():
        m_ref[...] = jnp.full(m_ref.shape, NEG_BIG, F32)
        acc_ref[...] = jnp.zeros(acc_ref.shape, F32)

    def update(k_of, v_of):
        for h in range(N_DIFF_HEADS):
            qq = _stack_maps(q_ref[:, h * hd:(h + 1) * hd])
            _softmax_step(qq, k_of(h), v_of(h), m_ref.at[h], acc_ref.at[h], None)

    @pl.when(j < n_past)
    def _():
        update(lambda h: pk_ref[0, 0, :, h, :].astype(BF16), lambda h: pv_ref[0, 0, :, h, :].astype(BF16))

    @pl.when(j == n_past)
    def _():
        update(lambda h: nk_ref[:, h * hd:(h + 1) * hd], lambda h: nv_ref[:, h * hd:(h + 1) * hd])
        lam = _diff_lambda(lq1_ref, lk1_ref, lq2_ref, lk2_ref, lambda_init)
        for h in range(N_DIFF_HEADS):
            o_ref[:, h * hd:(h + 1) * hd] = _diff_finish(acc_ref[h], lam, g_ref[...], lambda_init,
                                                         tq).astype(o_ref.dtype)


def _diff_sample(qb, past_k, past_v, kb, vb, lam_vecs, subln_g, batch, seq, lambda_init):
    past_len = past_k.shape[2]
    assert past_len % TK_PAST == 0
    n_past = past_len // TK_PAST
    w = DIFF_WIDTH
    small = lambda b, j: (0, 0)
    past_spec = pl.BlockSpec((1, 1, TK_PAST, N_DIFF_HEADS, DIFF_V_DIM),
                             lambda b, j: (0, b, jnp.minimum(j, n_past - 1), 0, 0))
    new_spec = pl.BlockSpec((seq, w), lambda b, j: (b, 0))
    return pl.pallas_call(
        functools.partial(_diff_sample_kernel, lambda_init=lambda_init, n_past=n_past),
        grid=(batch, n_past + 1),
        in_specs=[new_spec, past_spec, past_spec, new_spec, new_spec]
                 + [pl.BlockSpec((1, DIFF_HEAD_DIM), small)] * 4
                 + [pl.BlockSpec((1, DIFF_V_DIM), small)],
        out_specs=new_spec,
        out_shape=jax.ShapeDtypeStruct((batch * seq, w), BF16),
        scratch_shapes=[pltpu.VMEM((N_DIFF_HEADS, 2 * seq, 1), F32),
                        pltpu.VMEM((N_DIFF_HEADS, 2 * seq, 2 * DIFF_V_DIM), F32)],
        compiler_params=_cparams("parallel", "arbitrary"),
        name="diff_attn_sample",
    )(qb, past_k, past_v, kb, vb, *lam_vecs, subln_g)


def _post_kernel(d_ref, u_ref, hist_ref, x_ref, pw_ref, ps_ref, wo1_ref, g1_ref, b1_ref, wq_ref, mk_ref,
                 mv_ref, wo2_ref, g2_ref, b2_ref, rwh_ref, rwl_ref, rb_ref,
                 h2_ref, h2p_ref, idx_ref, gate_ref, full_ref, *, tm, seq, pos0, zero_first_hist):
    i = pl.program_id(0)
    row0 = (i * tm) % seq

    hist = hist_ref[...]
    if zero_first_hist:
        hist = jnp.where(row0 == 0, jnp.zeros_like(hist), hist)
    full_ref[0:HIST_ROWS, :] = hist
    u = u_ref[...]
    full_ref[HIST_ROWS:, :] = u
    pos = pos0 + row0 + lax.broadcasted_iota(I32, (tm, 1), 0)
    pooled = []
    for gi, w in enumerate(POOL_WINDOWS):
        cs = slice(gi * POOL_GROUP_DIM, (gi + 1) * POOL_GROUP_DIM)
        acc = u[:, cs]
        for back in range(1, w):
            acc = acc + full_ref[HIST_ROWS - back:HIST_ROWS - back + tm, cs]
        cnt = jnp.minimum(w, pos + 1).astype(F32)
        m = acc / cnt - u[:, cs]
        y = jnp.dot(m.astype(BF16), pw_ref[gi], preferred_element_type=F32)
        pooled.append((y * ps_ref[:, cs]).astype(BF16))
    mixed_in = jnp.concatenate([d_ref[...]] + pooled, axis=-1)

    mix = jnp.dot(mixed_in, wo1_ref[...], preferred_element_type=F32)
    h1 = _layer_norm(DEEPNORM_ALPHA * x_ref[...] + mix, g1_ref[...], b1_ref[...])

    q = jnp.dot(h1.astype(BF16), wq_ref[...], preferred_element_type=F32)
    qb = (q * (XA_HEAD_DIM ** -0.5)).astype(BF16)
    heads = []
    for h in range(N_XA_HEADS):
        cs = slice(h * XA_HEAD_DIM, (h + 1) * XA_HEAD_DIM)
        s = lax.dot_general(qb[:, cs], mk_ref[:, cs], (((1,), (1,)), ((), ())), preferred_element_type=F32)
        e = jnp.exp(s - jnp.max(s, axis=-1, keepdims=True))
        p = e / jnp.sum(e, axis=-1, keepdims=True)
        heads.append(jnp.dot(p.astype(BF16), mv_ref[:, cs], preferred_element_type=F32).astype(BF16))
    ca = jnp.dot(jnp.concatenate(heads, axis=-1), wo2_ref[...], preferred_element_type=F32)
    h2 = _layer_norm(DEEPNORM_ALPHA * h1 + ca, g2_ref[...], b2_ref[...])
    h2_ref[...] = h2

    hb = h2.astype(BF16)
    bits = pltpu.bitcast(hb.astype(F32), U32)
    half = D_MODEL // 2
    h2p_ref[:, 0, :] = (bits[:, :half] >> 16) | (bits[:, half:] & jnp.uint32(0xFFFF0000))

    hl = (h2 - hb.astype(F32)).astype(BF16)
    dn = (((1,), (1,)), ((), ()))
    logits = (lax.dot_general(rwh_ref[...], hb, dn, preferred_element_type=F32)
              + lax.dot_general(rwh_ref[...], hl, dn, preferred_element_type=F32)
              + lax.dot_general(rwl_ref[...], hb, dn, preferred_element_type=F32)
              + rb_ref[...])
    erow = lax.broadcasted_iota(I32, logits.shape, 0)
    vals, idxs = [], []
    for _ in range(TOP_K):
        mx = jnp.max(logits, axis=0, keepdims=True)
        ix = jnp.min(jnp.where(logits == mx, erow, N_EXPERTS), axis=0, keepdims=True)
        vals.append(mx)
        idxs.append(ix)
        logits = jnp.where(erow == ix, -jnp.inf, logits)
    ex = [jnp.exp(v - vals[0]) for v in vals]
    den = ex[0] + ex[1] + ex[2] + ex[3]
    idx_ref[0] = jnp.concatenate(idxs, axis=0)
    gate_ref[0] = jnp.concatenate([e / den for e in ex], axis=0)


def _post(diff_out, u, hist_arr, hist_map, x2d, mkb, mvb, wts, *, tm, seq, pos0, zero_first_hist):
    t = x2d.shape[0]
    n = t // tm
    assert seq % tm == 0 or tm % seq == 0
    row = lambda i: (i, 0)
    full2 = lambda i: (0, 0)
    mem_map = lambda i: ((i * tm) // seq, 0)
    vec = pl.BlockSpec((1, D_MODEL), full2)
    wspec = pl.BlockSpec((D_MODEL, D_MODEL), full2)
    in_specs = [
        pl.BlockSpec((tm, DIFF_WIDTH), row), pl.BlockSpec((tm, POOL_WIDTH), row),
        pl.BlockSpec((HIST_ROWS, POOL_WIDTH), hist_map), pl.BlockSpec((tm, D_MODEL), row),
        pl.BlockSpec((len(POOL_WINDOWS), POOL_GROUP_DIM, POOL_GROUP_DIM), lambda i: (0, 0, 0)),
        pl.BlockSpec((1, POOL_WIDTH), full2),
        wspec, vec, vec,
        wspec, pl.BlockSpec((N_MEM, D_MODEL), mem_map), pl.BlockSpec((N_MEM, D_MODEL), mem_map),
        wspec, vec, vec,
        pl.BlockSpec((N_EXPERTS, D_MODEL), full2), pl.BlockSpec((N_EXPERTS, D_MODEL), full2),
        pl.BlockSpec((N_EXPERTS, 1), full2),
    ]
    out_specs = [pl.BlockSpec((tm, D_MODEL), row), pl.BlockSpec((tm, 1, D_MODEL // 2), lambda i: (i, 0, 0)),
                 pl.BlockSpec((1, TOP_K, tm), lambda i: (i, 0, 0)),
                 pl.BlockSpec((1, TOP_K, tm), lambda i: (i, 0, 0))]
    out_shape = [jax.ShapeDtypeStruct((t, D_MODEL), F32), jax.ShapeDtypeStruct((t, 1, D_MODEL // 2), U32),
                 jax.ShapeDtypeStruct((n, TOP_K, tm), I32), jax.ShapeDtypeStruct((n, TOP_K, tm), F32)]
    return pl.pallas_call(
        functools.partial(_post_kernel, tm=tm, seq=seq, pos0=pos0, zero_first_hist=zero_first_hist),
        grid=(n,),
        in_specs=in_specs,
        out_specs=out_specs,
        out_shape=out_shape,
        scratch_shapes=[pltpu.VMEM((HIST_ROWS + tm, POOL_WIDTH), F32)],
        compiler_params=_cparams("parallel"),
        name="post_attn",
    )(diff_out, u, hist_arr, x2d, wts["pool_w"], wts["pool_scale"], wts["w_out"], wts["ln1_g"], wts["ln1_b"],
      wts["xa_wq"], mkb, mvb, wts["xa_wo"], wts["ln2_g"], wts["ln2_b"], wts["rw_hi"], wts["rw_lo"],
      wts["router_b"])


def _rank_kernel(idx_ref, rank_ref, cnt_ref, carry_ref):
    @pl.when(pl.program_id(0) == 0)
    def _():
        carry_ref[...] = jnp.zeros(carry_ref.shape, F32)

    idx = idx_ref[...]
    tr = idx.shape[1]
    erow = lax.broadcasted_iota(I32, (N_EXPERTS, tr), 0)
    hits = [erow == idx[k:k + 1, :] for k in range(TOP_K)]
    onehot = sum(h.astype(F32) for h in hits)
    earlier = (lax.broadcasted_iota(I32, (tr, tr), 0) < lax.broadcasted_iota(I32, (tr, tr), 1)).astype(BF16)
    before = jnp.dot(onehot.astype(BF16), earlier, preferred_element_type=F32) + carry_ref[:, 0:1]
    ranks = [jnp.sum(jnp.where(h, before, 0.0), axis=0, keepdims=True) for h in hits]
    rank_ref[...] = jnp.concatenate(ranks, axis=0).astype(I32)
    carry_ref[...] = carry_ref[...] + jnp.sum(onehot, axis=1, keepdims=True)
    cnt_ref[...] = carry_ref[...].astype(I32)


def _ranks(idx_all):
    t = idx_all.shape[1]
    assert t % TR_RANK == 0
    return pl.pallas_call(
        _rank_kernel,
        grid=(t // TR_RANK,),
        in_specs=[pl.BlockSpec((TOP_K, TR_RANK), lambda i: (0, i))],
        out_specs=[pl.BlockSpec((TOP_K, TR_RANK), lambda i: (0, i)),
                   pl.BlockSpec((N_EXPERTS, 128), lambda i: (0, 0))],
        out_shape=[jax.ShapeDtypeStruct((TOP_K, t), I32), jax.ShapeDtypeStruct((N_EXPERTS, 128), I32)],
        scratch_shapes=[pltpu.VMEM((N_EXPERTS, 128), F32)],
        compiler_params=_cparams("arbitrary"),
        name="moe_rank",
    )(idx_all)


def _dest_kernel(start_ref, idx_ref, rank_ref, dest_ref):
    idx = idx_ref[...]
    base = jnp.zeros(idx.shape, I32)
    for e in range(N_EXPERTS):
        base = jnp.where(idx == e, start_ref[e], base)
    dest_ref[...] = base + rank_ref[...]


def _dests(pad_start, idx_all, rank_all):
    t = idx_all.shape[1]
    spec = pl.BlockSpec((TOP_K, TR_RANK), lambda i, s: (0, i))
    return pl.pallas_call(
        _dest_kernel,
        grid_spec=pltpu.PrefetchScalarGridSpec(
            num_scalar_prefetch=1, grid=(t // TR_RANK,), in_specs=[spec, spec], out_specs=spec),
        out_shape=jax.ShapeDtypeStruct((TOP_K, t), I32),
        compiler_params=_cparams("parallel"),
        name="moe_dest",
    )(pad_start, idx_all, rank_all)


def _row_copy(src_ref, src_row, dst_ref, dst_row, sem):
    return pltpu.make_async_copy(src_ref.at[pl.ds(src_row, 1)], dst_ref.at[pl.ds(dst_row, 1)], sem)


def _dispatch_kernel(cnt_ref, start_ref, hp_hbm, hs_hbm, dest_hbm, xs_out, dest_smem, zero_ref,
                     sem_idx, sem_rows, sem_fill, *, n_prompt):
    i = pl.program_id(0)
    n = TM_MOE * TOP_K
    cp = pltpu.make_async_copy(dest_hbm.at[pl.ds(i * n, n)], dest_smem, sem_idx)
    cp.start()

    @pl.when(i == 0)
    def _():
        zero_ref[...] = jnp.zeros(zero_ref.shape, U32)
        for e in range(N_EXPERTS):
            n_pad = (cnt_ref[e] + MOE_BLOCK - 1) // MOE_BLOCK * MOE_BLOCK - cnt_ref[e]
            first = start_ref[e] + cnt_ref[e]

            def fill(r, c):
                _row_copy(zero_ref, 0, xs_out, first + r, sem_fill).start()
                return c

            lax.fori_loop(0, n_pad, fill, 0)

            def drain_fill(r, c):
                _row_copy(zero_ref, 0, xs_out, 0, sem_fill).wait()
                return c

            lax.fori_loop(0, n_pad, drain_fill, 0)

        last = N_EXPERTS - 1
        n_used = (start_ref[last] + cnt_ref[last] + MOE_BLOCK - 1) // MOE_BLOCK
        n_blocks = xs_out.shape[0] // MOE_BLOCK

        def block_copy(b):
            return pltpu.make_async_copy(zero_ref, xs_out.at[pl.ds(b * MOE_BLOCK, MOE_BLOCK)], sem_fill)

        def fill_block(b, c):
            block_copy(b).start()
            return c

        lax.fori_loop(n_used, n_blocks, fill_block, 0)

        def drain_block(b, c):
            block_copy(b).wait()
            return c

        lax.fori_loop(n_used, n_blocks, drain_block, 0)

    cp.wait()

    def scatter_from(src_hbm, row0):
        def issue(t, c):
            for k in range(TOP_K):
                _row_copy(src_hbm, row0 + t, xs_out, dest_smem[t * TOP_K + k], sem_rows).start()
            return c

        lax.fori_loop(0, TM_MOE, issue, 0, unroll=2)

    @pl.when(i < n_prompt)
    def _():
        scatter_from(hp_hbm, i * TM_MOE)

    @pl.when(i >= n_prompt)
    def _():
        scatter_from(hs_hbm, (i - n_prompt) * TM_MOE)

    tile_rows = xs_out.at[pl.ds(0, n)]
    wait_tile = pltpu.make_async_copy(tile_rows, tile_rows, sem_rows)

    @pl.when(i > 0)
    def _():
        wait_tile.wait()

    @pl.when(i == pl.num_programs(0) - 1)
    def _():
        wait_tile.wait()


def _dispatch(counts, pad_start, h2p_p, h2p_s, dest_flat, cap):
    n_prompt = h2p_p.shape[0] // TM_MOE
    n_sample = h2p_s.shape[0] // TM_MOE
    half = D_MODEL // 2
    hbm = pl.BlockSpec(memory_space=pl.ANY)
    return pl.pallas_call(
        functools.partial(_dispatch_kernel, n_prompt=n_prompt),
        grid_spec=pltpu.PrefetchScalarGridSpec(
            num_scalar_prefetch=2, grid=(n_prompt + n_sample,),
            in_specs=[hbm, hbm, hbm],
            out_specs=hbm,
            scratch_shapes=[pltpu.SMEM((TM_MOE * TOP_K,), I32), pltpu.VMEM((MOE_BLOCK, 1, half), U32),
                            pltpu.SemaphoreType.DMA, pltpu.SemaphoreType.DMA, pltpu.SemaphoreType.DMA]),
        out_shape=jax.ShapeDtypeStruct((cap, 1, half), U32),
        compiler_params=_cparams("arbitrary"),
        name="moe_dispatch",
    )(counts, pad_start, h2p_p, h2p_s, dest_flat)


def _expert_kernel(be_ref, act_ref, new_ref, x_ref, wg_ref, bg_ref, wu_ref, bu_ref, wd_ref, bd_ref, o_ref,
                   wgb_ref, wub_ref, wdb_ref):
    b = pl.program_id(0)

    @pl.when(new_ref[b] == 1)
    def _():
        wgb_ref[...] = wg_ref[0].astype(BF16)
        wub_ref[...] = wu_ref[0].astype(BF16)
        wdb_ref[...] = wd_ref[0].astype(BF16)

    @pl.when(act_ref[b] == 1)
    def _():
        p = x_ref[:, 0, :]
        lo = pltpu.bitcast(p << 16, F32)
        hi = pltpu.bitcast(p & jnp.uint32(0xFFFF0000), F32)
        x = jnp.concatenate([lo, hi], axis=-1).astype(BF16)
        g = jnp.dot(x, wgb_ref[...], preferred_element_type=F32) + bg_ref[0]
        u = jnp.dot(x, wub_ref[...], preferred_element_type=F32) + bu_ref[0]
        g = jnp.minimum(g, SWIGLU_LIMIT)
        u = jnp.clip(u, -SWIGLU_LIMIT, SWIGLU_LIMIT)
        a = (u + 1.0) * (g * jax.nn.sigmoid(SWIGLU_ALPHA * g))
        o_ref[:, 0, :] = jnp.dot(a.astype(BF16), wdb_ref[...], preferred_element_type=F32) + bd_ref[0]

    @pl.when(act_ref[b] == 0)
    def _():
        o_ref[...] = jnp.zeros(o_ref.shape, F32)


def _experts(block_expert, block_active, block_new, xs, wg, bg, wu, bu, wd, bd):
    cap = xs.shape[0]
    n_blocks = cap // MOE_BLOCK
    wspec = pl.BlockSpec((1, D_MODEL, D_MODEL), lambda b, be, act, new: (be[b], 0, 0))
    bspec = pl.BlockSpec((1, 1, D_MODEL), lambda b, be, act, new: (be[b], 0, 0))
    return pl.pallas_call(
        _expert_kernel,
        grid_spec=pltpu.PrefetchScalarGridSpec(
            num_scalar_prefetch=3, grid=(n_blocks,),
            in_specs=[pl.BlockSpec((MOE_BLOCK, 1, D_MODEL // 2), lambda b, be, act, new: (b * act[b], 0, 0)),
                      wspec, bspec, wspec, bspec, wspec, bspec],
            out_specs=pl.BlockSpec((MOE_BLOCK, 1, D_MODEL), lambda b, be, act, new: (b, 0, 0)),
            scratch_shapes=[pltpu.VMEM((D_MODEL, D_MODEL), BF16)] * 3),
        out_shape=jax.ShapeDtypeStruct((cap, 1, D_MODEL), F32),
        compiler_params=_cparams("arbitrary"),
        name="moe_experts",
    )(block_expert, block_active, block_new, xs, wg, bg, wu, bu, wd, bd)


def _combine_kernel(hp_ref, hs_ref, gate_ref, dest_hbm, yb_hbm, g_ref, b_ref, yp_ref, ys_ref,
                    dest_smem, rows_ref, sem_idx, sem_rows, *, n_prompt):
    i = pl.program_id(0)
    n = TM_MOE * TOP_K
    slot = i % 2

    def gather_tile(tile, s):
        cp = pltpu.make_async_copy(dest_hbm.at[pl.ds(tile * n, n)], dest_smem.at[pl.ds(s * n, n)], sem_idx)
        cp.start()
        cp.wait()

        def issue(t, c):
            for k in range(TOP_K):
                _row_copy(yb_hbm, dest_smem[s * n + t * TOP_K + k], rows_ref.at[s, k], t, sem_rows.at[s]).start()
            return c

        lax.fori_loop(0, TM_MOE, issue, 0, unroll=2)

    @pl.when(i == 0)
    def _():
        gather_tile(0, 0)

    @pl.when(i + 1 < pl.num_programs(0))
    def _():
        gather_tile(i + 1, 1 - slot)

    pltpu.make_async_copy(rows_ref.at[slot], rows_ref.at[slot], sem_rows.at[slot]).wait()

    gate = gate_ref[...]
    ff = gate[:, 0:1] * rows_ref[slot, 0, :, 0, :]
    for k in range(1, TOP_K):
        ff = ff + gate[:, k:k + 1] * rows_ref[slot, k, :, 0, :]

    @pl.when(i < n_prompt)
    def _():
        yp_ref[...] = _layer_norm(DEEPNORM_ALPHA * hp_ref[...] + ff, g_ref[...], b_ref[...])

    @pl.when(i >= n_prompt)
    def _():
        ys_ref[...] = _layer_norm(DEEPNORM_ALPHA * hs_ref[...] + ff, g_ref[...], b_ref[...])


def _combine(h2_p, h2_s, gates, dest_flat, yb, ln_g, ln_b):
    n_prompt = h2_p.shape[0] // TM_MOE
    n_sample = h2_s.shape[0] // TM_MOE
    pmap = lambda i: (jnp.minimum(i, n_prompt - 1), 0)
    smap = lambda i: (jnp.maximum(i - n_prompt, 0), 0)
    vec = pl.BlockSpec((1, D_MODEL), lambda i: (0, 0))
    return pl.pallas_call(
        functools.partial(_combine_kernel, n_prompt=n_prompt),
        grid=(n_prompt + n_sample,),
        in_specs=[pl.BlockSpec((TM_MOE, D_MODEL), pmap), pl.BlockSpec((TM_MOE, D_MODEL), smap),
                  pl.BlockSpec((TM_MOE, TOP_K), lambda i: (i, 0)),
                  pl.BlockSpec(memory_space=pl.ANY), pl.BlockSpec(memory_space=pl.ANY), vec, vec],
        out_specs=[pl.BlockSpec((TM_MOE, D_MODEL), pmap), pl.BlockSpec((TM_MOE, D_MODEL), smap)],
        out_shape=[jax.ShapeDtypeStruct(h2_p.shape, F32), jax.ShapeDtypeStruct(h2_s.shape, F32)],
        scratch_shapes=[pltpu.SMEM((2 * TM_MOE * TOP_K,), I32),
                        pltpu.VMEM((2, TOP_K, TM_MOE, 1, D_MODEL), F32),
                        pltpu.SemaphoreType.DMA, pltpu.SemaphoreType.DMA((2,))],
        compiler_params=_cparams("arbitrary"),
        name="moe_combine",
    )(h2_p, h2_s, gates, dest_flat, yb, ln_g, ln_b)


def _tokens_major(a):
    n, k, tm = a.shape
    return jnp.transpose(a, (1, 0, 2)).reshape(k, n * tm)


def kernel(x_prompt, x_sample, mem_prompt, cache_diff_k, cache_diff_v, state_pool, cache_mem_k, cache_mem_v, w_in, lambda_q1, lambda_k1, lambda_q2, lambda_k2, subln_g, pool_w, pool_scale, w_out, ln1_g, ln1_b, xa_wq, xa_wk, xa_wv, xa_wo, ln2_g, ln2_b, router_w, router_b, moe_w_gate, moe_b_gate, moe_w_up, moe_b_up, moe_w_down, moe_b_down, ln3_g, ln3_b):
    assert w_in.shape[0] == DEPTH == 1
    batch, seq, d = x_prompt.shape
    dec_batch, dec_seq, _ = x_sample.shape
    past_len = cache_diff_k.shape[2]
    lambda_init = 0.8 - 0.6 * math.exp(-0.3 * 0)
    tp, ts = batch * seq, dec_batch * dec_seq

    vec = lambda a: a[0].reshape(1, -1)
    rw_t = router_w[0].T
    rw_hi = rw_t.astype(BF16)
    wts = dict(
        pool_w=pool_w[0].astype(BF16), pool_scale=vec(pool_scale), w_out=w_out[0].astype(BF16),
        ln1_g=vec(ln1_g), ln1_b=vec(ln1_b), xa_wq=xa_wq[0].astype(BF16), xa_wo=xa_wo[0].astype(BF16),
        ln2_g=vec(ln2_g), ln2_b=vec(ln2_b), rw_hi=rw_hi, rw_lo=(rw_t - rw_hi.astype(F32)).astype(BF16),
        router_b=router_b[0].reshape(N_EXPERTS, 1))
    w_in_bf = w_in[0].astype(BF16)
    lam_vecs = [vec(lambda_q1), vec(lambda_k1), vec(lambda_q2), vec(lambda_k2)]
    g_sub = vec(subln_g)

    xp2 = x_prompt.reshape(tp, d)
    qp, kp, vp, up, kpb, vpb = _inproj(xp2, w_in_bf, TM_PROJ)
    dp = _diff_prompt(qp, kpb, vpb, lam_vecs, g_sub, batch, seq, lambda_init)
    mk, mv, mkb, mvb = _memkv(mem_prompt.reshape(batch * N_MEM, d), xa_wk[0].astype(BF16),
                              xa_wv[0].astype(BF16), TM_PROJ)
    per16 = TM_POST // HIST_ROWS
    h2_p, h2p_p, idx_p, gate_p = _post(
        dp, up, up, lambda i: (jnp.maximum(i * per16 - 1, 0), 0), xp2, mkb, mvb, wts,
        tm=TM_POST, seq=seq, pos0=0, zero_first_hist=True)

    xs2 = x_sample.reshape(ts, d)
    qs, ks, vs, us, ksb, vsb = _inproj(xs2, w_in_bf, TM_PROJ)
    ds = _diff_sample(qs, cache_diff_k, cache_diff_v, ksb, vsb, lam_vecs, g_sub, dec_batch, dec_seq, lambda_init)
    hist_s = jnp.pad(state_pool[0], ((0, 0), (1, 0), (0, 0))).reshape(dec_batch * HIST_ROWS, POOL_WIDTH)
    h2_s, h2p_s, idx_s, gate_s = _post(
        ds, us, hist_s, lambda i: (i, 0), xs2,
        cache_mem_k[0].reshape(dec_batch * N_MEM, d).astype(BF16),
        cache_mem_v[0].reshape(dec_batch * N_MEM, d).astype(BF16), wts,
        tm=dec_seq, seq=dec_seq, pos0=past_len, zero_first_hist=False)

    t_all = tp + ts
    idx_all = jnp.concatenate([_tokens_major(idx_p), _tokens_major(idx_s)], axis=1)
    gate_all = jnp.concatenate([_tokens_major(gate_p), _tokens_major(gate_s)], axis=1)
    rank_all, counts = _ranks(idx_all)
    counts = counts[:, 0]
    padded = (counts + MOE_BLOCK - 1) // MOE_BLOCK * MOE_BLOCK
    pad_end = jnp.cumsum(padded)
    pad_start = (pad_end - padded).astype(I32)
    n_blocks = t_all * TOP_K // MOE_BLOCK + N_EXPERTS
    block_start = jnp.arange(n_blocks, dtype=I32) * MOE_BLOCK
    block_expert = jnp.minimum(jnp.sum(block_start[:, None] >= pad_end[None, :], axis=1), N_EXPERTS - 1).astype(I32)
    block_active = (block_start < pad_end[-1]).astype(I32)
    block_new = jnp.concatenate([jnp.ones((1,), I32), (block_expert[1:] != block_expert[:-1]).astype(I32)])
    dest_flat = _dests(pad_start, idx_all, rank_all).T.reshape(-1)
    xs = _dispatch(counts, pad_start, h2p_p, h2p_s, dest_flat, n_blocks * MOE_BLOCK)
    yb = _experts(block_expert, block_active, block_new, xs,
                  moe_w_gate[0], moe_b_gate[0].reshape(N_EXPERTS, 1, D_MODEL),
                  moe_w_up[0], moe_b_up[0].reshape(N_EXPERTS, 1, D_MODEL),
                  moe_w_down[0], moe_b_down[0].reshape(N_EXPERTS, 1, D_MODEL))
    y_p, y_s = _combine(h2_p, h2_s, gate_all.T, dest_flat, yb, vec(ln3_g), vec(ln3_b))

    heads = (N_DIFF_HEADS, DIFF_V_DIM)
    xa = (N_XA_HEADS, XA_HEAD_DIM)
    up3 = up.reshape(batch, seq, POOL_WIDTH)
    us3 = us.reshape(dec_batch, dec_seq, POOL_WIDTH)
    pool_s = jnp.concatenate([state_pool[0].astype(F32), us3], axis=1)[:, -POOL_HIST:]
    return (y_p.reshape(batch, seq, d), y_s.reshape(dec_batch, dec_seq, d),
            kp.reshape(1, batch, seq, *heads), vp.reshape(1, batch, seq, *heads),
            up3[:, seq - POOL_HIST:][None],
            mk.reshape(1, batch, N_MEM, *xa), mv.reshape(1, batch, N_MEM, *xa),
            ks.reshape(1, dec_batch, dec_seq, *heads), vs.reshape(1, dec_batch, dec_seq, *heads),
            pool_s[None])
```

```python
import functools
import math

import jax
import jax.numpy as jnp
from jax import lax
from jax.experimental import pallas as pl
from jax.experimental.pallas import tpu as pltpu

F32 = jnp.float32
BF16 = jnp.bfloat16
I32 = jnp.int32
U32 = jnp.uint32

D_MODEL = 1024
CHUNK = 64
N_DIFF_HEADS = 4
DIFF_HEAD_DIM = 64
DIFF_V_DIM = 2 * DIFF_HEAD_DIM
DIFF_WIDTH = N_DIFF_HEADS * DIFF_V_DIM
POOL_WIDTH = D_MODEL - DIFF_WIDTH
POOL_WINDOWS = (2, 4, 8, 16)
POOL_GROUP_DIM = POOL_WIDTH // len(POOL_WINDOWS)
POOL_HIST = max(POOL_WINDOWS) - 1
HIST_ROWS = POOL_HIST + 1
IN_WIDTH = 3 * DIFF_WIDTH + POOL_WIDTH
N_MEM = 256
N_XA_HEADS = 4
XA_HEAD_DIM = D_MODEL // N_XA_HEADS
N_EXPERTS = 32
TOP_K = 4
SWIGLU_LIMIT = 7.0
SWIGLU_ALPHA = 1.702
MOE_BLOCK = 256
LN_EPS = 1e-5
RMS_EPS = 1e-5
DEPTH = 1
DEEPNORM_ALPHA = (2.0 * DEPTH) ** 0.25
NEG_BIG = -1e30

VMEM_LIMIT = 56 * 1024 * 1024

TM_PROJ = 512
TQ_ATT = 256
TK_ATT = 512
TK_PAST = 1024
TM_POST = 512
TR_RANK = 512
TM_MOE = 256


def _cparams(*sem):
    return pltpu.CompilerParams(dimension_semantics=sem, vmem_limit_bytes=VMEM_LIMIT)


def _layer_norm(x, g, b):
    mu = jnp.mean(x, axis=-1, keepdims=True)
    xc = x - mu
    var = jnp.mean(xc * xc, axis=-1, keepdims=True)
    return xc * lax.rsqrt(var + LN_EPS) * g + b


def _inproj_kernel(x_ref, w_ref, q_ref, k_ref, v_ref, u_ref, kb_ref, vb_ref):
    x = x_ref[...].astype(BF16)
    w = DIFF_WIDTH
    q = jnp.dot(x, w_ref[:, 0:w], preferred_element_type=F32)
    q_ref[...] = (q * (DIFF_HEAD_DIM ** -0.5)).astype(BF16)
    k = jnp.dot(x, w_ref[:, w:2 * w], preferred_element_type=F32)
    kb_ref[...] = k.astype(BF16)
    v = jnp.dot(x, w_ref[:, 2 * w:3 * w], preferred_element_type=F32)
    vb_ref[...] = v.astype(BF16)
    for h in range(N_DIFF_HEADS):
        cs = slice(h * DIFF_V_DIM, (h + 1) * DIFF_V_DIM)
        k_ref[:, h, :] = k[:, cs]
        v_ref[:, h, :] = v[:, cs]
    u_ref[...] = jnp.dot(x, w_ref[:, 3 * w:], preferred_element_type=F32)


def _inproj(x2d, w_in_bf, tm):
    t = x2d.shape[0]
    w = DIFF_WIDTH
    row = lambda i: (i, 0)
    out_spec = pl.BlockSpec((tm, w), row)
    head_spec = pl.BlockSpec((tm, N_DIFF_HEADS, DIFF_V_DIM), lambda i: (i, 0, 0))
    head_shape = jax.ShapeDtypeStruct((t, N_DIFF_HEADS, DIFF_V_DIM), F32)
    return pl.pallas_call(
        _inproj_kernel,
        grid=(t // tm,),
        in_specs=[pl.BlockSpec((tm, D_MODEL), row),
                  pl.BlockSpec((D_MODEL, IN_WIDTH), lambda i: (0, 0))],
        out_specs=[out_spec, head_spec, head_spec, out_spec, out_spec, out_spec],
        out_shape=[jax.ShapeDtypeStruct((t, w), BF16), head_shape, head_shape,
                   jax.ShapeDtypeStruct((t, w), F32),
                   jax.ShapeDtypeStruct((t, w), BF16), jax.ShapeDtypeStruct((t, w), BF16)],
        compiler_params=_cparams("parallel"),
        name="inproj",
    )(x2d, w_in_bf)


def _memkv_kernel(x_ref, wk_ref, wv_ref, k_ref, v_ref, kb_ref, vb_ref):
    x = x_ref[...].astype(BF16)
    k = jnp.dot(x, wk_ref[...], preferred_element_type=F32)
    kb_ref[...] = k.astype(BF16)
    v = jnp.dot(x, wv_ref[...], preferred_element_type=F32)
    vb_ref[...] = v.astype(BF16)
    for h in range(N_XA_HEADS):
        cs = slice(h * XA_HEAD_DIM, (h + 1) * XA_HEAD_DIM)
        k_ref[:, h, :] = k[:, cs]
        v_ref[:, h, :] = v[:, cs]


def _memkv(mem2d, wk_bf, wv_bf, tm):
    t = mem2d.shape[0]
    row = lambda i: (i, 0)
    full = lambda i: (0, 0)
    spec = pl.BlockSpec((tm, D_MODEL), row)
    head_spec = pl.BlockSpec((tm, N_XA_HEADS, XA_HEAD_DIM), lambda i: (i, 0, 0))
    head_shape = jax.ShapeDtypeStruct((t, N_XA_HEADS, XA_HEAD_DIM), F32)
    return pl.pallas_call(
        _memkv_kernel,
        grid=(t // tm,),
        in_specs=[spec, pl.BlockSpec((D_MODEL, D_MODEL), full), pl.BlockSpec((D_MODEL, D_MODEL), full)],
        out_specs=[head_spec, head_spec, spec, spec],
        out_shape=[head_shape, head_shape,
                   jax.ShapeDtypeStruct((t, D_MODEL), BF16), jax.ShapeDtypeStruct((t, D_MODEL), BF16)],
        compiler_params=_cparams("parallel"),
        name="memkv",
    )(mem2d, wk_bf, wv_bf)


def _diff_lambda(lq1_ref, lk1_ref, lq2_ref, lk2_ref, lambda_init):
    s1 = jnp.sum(lq1_ref[...] * lk1_ref[...], axis=-1, keepdims=True)
    s2 = jnp.sum(lq2_ref[...] * lk2_ref[...], axis=-1, keepdims=True)
    return jnp.exp(s1) - jnp.exp(s2) + lambda_init


def _stack_maps(qh):
    lane = lax.broadcasted_iota(I32, qh.shape, 1)
    zero = jnp.zeros_like(qh)
    return jnp.concatenate([jnp.where(lane < DIFF_HEAD_DIM, qh, zero),
                            jnp.where(lane >= DIFF_HEAD_DIM, qh, zero)], axis=0)


def _softmax_step(qq, kh, vh, m_ref, acc_ref, mask):
    s = lax.dot_general(qq, kh, (((1,), (1,)), ((), ())), preferred_element_type=F32)
    if mask is not None:
        s = jnp.where(mask, s, NEG_BIG)
    m_old = m_ref[...]
    m_new = jnp.maximum(m_old, jnp.max(s, axis=-1, keepdims=True))
    p = jnp.exp(s - m_new).astype(BF16)
    ones_col = (lax.broadcasted_iota(I32, vh.shape, 1) == 0).astype(BF16)
    pv = jnp.dot(p, jnp.concatenate([vh, ones_col], axis=1), preferred_element_type=F32)
    acc_ref[...] = jnp.exp(m_old - m_new) * acc_ref[...] + pv
    m_ref[...] = m_new


def _diff_finish(acc, lam, g, lambda_init, tq):
    hd = DIFF_V_DIM
    o = acc[:tq, :hd] / acc[:tq, hd:hd + 1] - lam * (acc[tq:, :hd] / acc[tq:, hd:hd + 1])
    o = o * lax.rsqrt(jnp.mean(o * o, axis=-1, keepdims=True) + RMS_EPS)
    return o * g * (1.0 - lambda_init)


def _diff_prompt_kernel(q_ref, k_ref, v_ref, lq1_ref, lk1_ref, lq2_ref, lk2_ref, g_ref, o_ref, m_ref, acc_ref,
                        *, lambda_init):
    i = pl.program_id(1)
    tq, tk, hd = TQ_ATT, TK_ATT, DIFF_V_DIM
    n_full = (i * tq) // tk
    qrow = lax.broadcasted_iota(I32, (2 * tq, tk), 0)
    qrow = jnp.where(qrow >= tq, qrow - tq, qrow) + i * tq
    kcol = lax.broadcasted_iota(I32, (2 * tq, tk), 1) + n_full * tk
    last_mask = (kcol // CHUNK) <= (qrow // CHUNK)
    m_ref[...] = jnp.full(m_ref.shape, NEG_BIG, F32)
    acc_ref[...] = jnp.zeros(acc_ref.shape, F32)

    def step(j, mask):
        r0 = pl.multiple_of(j * tk, tk)
        for h in range(N_DIFF_HEADS):
            cs = slice(h * hd, (h + 1) * hd)
            _softmax_step(_stack_maps(q_ref[:, cs]), k_ref[pl.ds(r0, tk), cs], v_ref[pl.ds(r0, tk), cs],
                          m_ref.at[h], acc_ref.at[h], mask)

    def body(j, c):
        step(j, None)
        return c

    lax.fori_loop(0, n_full, body, 0)
    step(n_full, last_mask)
    lam = _diff_lambda(lq1_ref, lk1_ref, lq2_ref, lk2_ref, lambda_init)
    for h in range(N_DIFF_HEADS):
        o_ref[:, h * hd:(h + 1) * hd] = _diff_finish(acc_ref[h], lam, g_ref[...], lambda_init,
                                                     tq).astype(o_ref.dtype)


def _diff_prompt(qb, kb, vb, lam_vecs, subln_g, batch, seq, lambda_init):
    assert TQ_ATT % CHUNK == 0 and TK_ATT % TQ_ATT == 0 and seq % TK_ATT == 0
    nq = seq // TQ_ATT
    w = DIFF_WIDTH
    small = lambda b, i: (0, 0)
    return pl.pallas_call(
        functools.partial(_diff_prompt_kernel, lambda_init=lambda_init),
        grid=(batch, nq),
        in_specs=[pl.BlockSpec((TQ_ATT, w), lambda b, i: (b * nq + i, 0)),
                  pl.BlockSpec((seq, w), lambda b, i: (b, 0)),
                  pl.BlockSpec((seq, w), lambda b, i: (b, 0))]
                 + [pl.BlockSpec((1, DIFF_HEAD_DIM), small)] * 4
                 + [pl.BlockSpec((1, DIFF_V_DIM), small)],
        out_specs=pl.BlockSpec((TQ_ATT, w), lambda b, i: (b * nq + i, 0)),
        out_shape=jax.ShapeDtypeStruct((batch * seq, w), BF16),
        scratch_shapes=[pltpu.VMEM((N_DIFF_HEADS, 2 * TQ_ATT, 1), F32),
                        pltpu.VMEM((N_DIFF_HEADS, 2 * TQ_ATT, 2 * DIFF_V_DIM), F32)],
        compiler_params=_cparams("parallel", "arbitrary"),
        name="diff_attn_prompt",
    )(qb, kb, vb, *lam_vecs, subln_g)


def _diff_sample_kernel(q_ref, pk_ref, pv_ref, nk_ref, nv_ref, lq1_ref, lk1_ref, lq2_ref, lk2_ref, g_ref,
                        o_ref, m_ref, acc_ref, *, lambda_init, n_past):
    j = pl.program_id(1)
    tq, hd = q_ref.shape[0], DIFF_V_DIM

    @pl.when(j == 0)
    def _():
        m_ref[...] = jnp.full(m_ref.shape, NEG_BIG, F32)
        acc_ref[...] = jnp.zeros(acc_ref.shape, F32)

    def update(k_of, v_of):
        for h in range(N_DIFF_HEADS):
            qq = _stack_maps(q_ref[:, h * hd:(h + 1) * hd])
            _softmax_step(qq, k_of(h), v_of(h), m_ref.at[h], acc_ref.at[h], None)

    @pl.when(j < n_past)
    def _():
        update(lambda h: pk_ref[0, 0, :, h, :].astype(BF16), lambda h: pv_ref[0, 0, :, h, :].astype(BF16))

    @pl.when(j == n_past)
    def _():
        update(lambda h: nk_ref[:, h * hd:(h + 1) * hd], lambda h: nv_ref[:, h * hd:(h + 1) * hd])
        lam = _diff_lambda(lq1_ref, lk1_ref, lq2_ref, lk2_ref, lambda_init)
        for h in range(N_DIFF_HEADS):
            o_ref[:, h * hd:(h + 1) * hd] = _diff_finish(acc_ref[h], lam, g_ref[...], lambda_init,
                                                         tq).astype(o_ref.dtype)


def _diff_sample(qb, past_k, past_v, kb, vb, lam_vecs, subln_g, batch, seq, lambda_init):
    past_len = past_k.shape[2]
    assert past_len % TK_PAST == 0
    n_past = past_len // TK_PAST
    w = DIFF_WIDTH
    small = lambda b, j: (0, 0)
    past_spec = pl.BlockSpec((1, 1, TK_PAST, N_DIFF_HEADS, DIFF_V_DIM),
                             lambda b, j: (0, b, jnp.minimum(j, n_past - 1), 0, 0))
    new_spec = pl.BlockSpec((seq, w), lambda b, j: (b, 0))
    return pl.pallas_call(
        functools.partial(_diff_sample_kernel, lambda_init=lambda_init, n_past=n_past),
        grid=(batch, n_past + 1),
        in_specs=[new_spec, past_spec, past_spec, new_spec, new_spec]
                 + [pl.BlockSpec((1, DIFF_HEAD_DIM), small)] * 4
                 + [pl.BlockSpec((1, DIFF_V_DIM), small)],
        out_specs=new_spec,
        out_shape=jax.ShapeDtypeStruct((batch * seq, w), BF16),
        scratch_shapes=[pltpu.VMEM((N_DIFF_HEADS, 2 * seq, 1), F32),
                        pltpu.VMEM((N_DIFF_HEADS, 2 * seq, 2 * DIFF_V_DIM), F32)],
        compiler_params=_cparams("parallel", "arbitrary"),
        name="diff_attn_sample",
    )(qb, past_k, past_v, kb, vb, *lam_vecs, subln_g)


def _post_kernel(d_ref, u_ref, hist_ref, x_ref, pw_ref, ps_ref, wo1_ref, g1_ref, b1_ref, wq_ref, mk_ref,
                 mv_ref, wo2_ref, g2_ref, b2_ref, rwh_ref, rwl_ref, rb_ref,
                 h2_ref, h2p_ref, idx_ref, gate_ref, full_ref, *, tm, seq, pos0, zero_first_hist):
    i = pl.program_id(0)
    row0 = (i * tm) % seq

    hist = hist_ref[...]
    if zero_first_hist:
        hist = jnp.where(row0 == 0, jnp.zeros_like(hist), hist)
    full_ref[0:HIST_ROWS, :] = hist
    u = u_ref[...]
    full_ref[HIST_ROWS:, :] = u
    pos = pos0 + row0 + lax.broadcasted_iota(I32, (tm, 1), 0)
    pooled = []
    for gi, w in enumerate(POOL_WINDOWS):
        cs = slice(gi * POOL_GROUP_DIM, (gi + 1) * POOL_GROUP_DIM)
        acc = u[:, cs]
        for back in range(1, w):
            acc = acc + full_ref[HIST_ROWS - back:HIST_ROWS - back + tm, cs]
        cnt = jnp.minimum(w, pos + 1).astype(F32)
        m = acc / cnt - u[:, cs]
        y = jnp.dot(m.astype(BF16), pw_ref[gi], preferred_element_type=F32)
        pooled.append((y * ps_ref[:, cs]).astype(BF16))
    mixed_in = jnp.concatenate([d_ref[...]] + pooled, axis=-1)

    mix = jnp.dot(mixed_in, wo1_ref[...], preferred_element_type=F32)
    h1 = _layer_norm(DEEPNORM_ALPHA * x_ref[...] + mix, g1_ref[...], b1_ref[...])

    q = jnp.dot(h1.astype(BF16), wq_ref[...], preferred_element_type=F32)
    qb = (q * (XA_HEAD_DIM ** -0.5)).astype(BF16)
    heads = []
    for h in range(N_XA_HEADS):
        cs = slice(h * XA_HEAD_DIM, (h + 1) * XA_HEAD_DIM)
        s = lax.dot_general(qb[:, cs], mk_ref[:, cs], (((1,), (1,)), ((), ())), preferred_element_type=F32)
        e = jnp.exp(s - jnp.max(s, axis=-1, keepdims=True))
        p = e / jnp.sum(e, axis=-1, keepdims=True)
        heads.append(jnp.dot(p.astype(BF16), mv_ref[:, cs], preferred_element_type=F32).astype(BF16))
    ca = jnp.dot(jnp.concatenate(heads, axis=-1), wo2_ref[...], preferred_element_type=F32)
    h2 = _layer_norm(DEEPNORM_ALPHA * h1 + ca, g2_ref[...], b2_ref[...])
    h2_ref[...] = h2

    hb = h2.astype(BF16)
    bits = pltpu.bitcast(hb.astype(F32), U32)
    half = D_MODEL // 2
    h2p_ref[...] = (bits[:, :half] >> 16) | (bits[:, half:] & jnp.uint32(0xFFFF0000))

    hl = (h2 - hb.astype(F32)).astype(BF16)
    dn = (((1,), (1,)), ((), ()))
    logits = (lax.dot_general(rwh_ref[...], hb, dn, preferred_element_type=F32)
              + lax.dot_general(rwh_ref[...], hl, dn, preferred_element_type=F32)
              + lax.dot_general(rwl_ref[...], hb, dn, preferred_element_type=F32)
              + rb_ref[...])
    erow = lax.broadcasted_iota(I32, logits.shape, 0)
    vals, idxs = [], []
    for _ in range(TOP_K):
        mx = jnp.max(logits, axis=0, keepdims=True)
        ix = jnp.min(jnp.where(logits == mx, erow, N_EXPERTS), axis=0, keepdims=True)
        vals.append(mx)
        idxs.append(ix)
        logits = jnp.where(erow == ix, -jnp.inf, logits)
    ex = [jnp.exp(v - vals[0]) for v in vals]
    den = ex[0] + ex[1] + ex[2] + ex[3]
    idx_ref[0] = jnp.concatenate(idxs, axis=0)
    gate_ref[0] = jnp.concatenate([e / den for e in ex], axis=0)


def _post(diff_out, u, hist_arr, hist_map, x2d, mkb, mvb, wts, *, tm, seq, pos0, zero_first_hist):
    t = x2d.shape[0]
    n = t // tm
    assert seq % tm == 0 or tm % seq == 0
    row = lambda i: (i, 0)
    full2 = lambda i: (0, 0)
    mem_map = lambda i: ((i * tm) // seq, 0)
    vec = pl.BlockSpec((1, D_MODEL), full2)
    wspec = pl.BlockSpec((D_MODEL, D_MODEL), full2)
    in_specs = [
        pl.BlockSpec((tm, DIFF_WIDTH), row), pl.BlockSpec((tm, POOL_WIDTH), row),
        pl.BlockSpec((HIST_ROWS, POOL_WIDTH), hist_map), pl.BlockSpec((tm, D_MODEL), row),
        pl.BlockSpec((len(POOL_WINDOWS), POOL_GROUP_DIM, POOL_GROUP_DIM), lambda i: (0, 0, 0)),
        pl.BlockSpec((1, POOL_WIDTH), full2),
        wspec, vec, vec,
        wspec, pl.BlockSpec((N_MEM, D_MODEL), mem_map), pl.BlockSpec((N_MEM, D_MODEL), mem_map),
        wspec, vec, vec,
        pl.BlockSpec((N_EXPERTS, D_MODEL), full2), pl.BlockSpec((N_EXPERTS, D_MODEL), full2),
        pl.BlockSpec((N_EXPERTS, 1), full2),
    ]
    out_specs = [pl.BlockSpec((tm, D_MODEL), row), pl.BlockSpec((tm, D_MODEL // 2), row),
                 pl.BlockSpec((1, TOP_K, tm), lambda i: (i, 0, 0)),
                 pl.BlockSpec((1, TOP_K, tm), lambda i: (i, 0, 0))]
    out_shape = [jax.ShapeDtypeStruct((t, D_MODEL), F32), jax.ShapeDtypeStruct((t, D_MODEL // 2), U32),
                 jax.ShapeDtypeStruct((n, TOP_K, tm), I32), jax.ShapeDtypeStruct((n, TOP_K, tm), F32)]
    return pl.pallas_call(
        functools.partial(_post_kernel, tm=tm, seq=seq, pos0=pos0, zero_first_hist=zero_first_hist),
        grid=(n,),
        in_specs=in_specs,
        out_specs=out_specs,
        out_shape=out_shape,
        scratch_shapes=[pltpu.VMEM((HIST_ROWS + tm, POOL_WIDTH), F32)],
        compiler_params=_cparams("parallel"),
        name="post_attn",
    )(diff_out, u, hist_arr, x2d, wts["pool_w"], wts["pool_scale"], wts["w_out"], wts["ln1_g"], wts["ln1_b"],
      wts["xa_wq"], mkb, mvb, wts["xa_wo"], wts["ln2_g"], wts["ln2_b"], wts["rw_hi"], wts["rw_lo"],
      wts["router_b"])


def _rank_kernel(idx_ref, rank_ref, cnt_ref, carry_ref):
    @pl.when(pl.program_id(0) == 0)
    def _():
        carry_ref[...] = jnp.zeros(carry_ref.shape, F32)

    idx = idx_ref[...]
    tr = idx.shape[1]
    erow = lax.broadcasted_iota(I32, (N_EXPERTS, tr), 0)
    hits = [erow == idx[k:k + 1, :] for k in range(TOP_K)]
    onehot = sum(h.astype(F32) for h in hits)
    earlier = (lax.broadcasted_iota(I32, (tr, tr), 0) < lax.broadcasted_iota(I32, (tr, tr), 1)).astype(BF16)
    before = jnp.dot(onehot.astype(BF16), earlier, preferred_element_type=F32) + carry_ref[:, 0:1]
    ranks = [jnp.sum(jnp.where(h, before, 0.0), axis=0, keepdims=True) for h in hits]
    rank_ref[...] = jnp.concatenate(ranks, axis=0).astype(I32)
    carry_ref[...] = carry_ref[...] + jnp.sum(onehot, axis=1, keepdims=True)
    cnt_ref[...] = carry_ref[...].astype(I32)


def _ranks(idx_all):
    t = idx_all.shape[1]
    assert t % TR_RANK == 0
    return pl.pallas_call(
        _rank_kernel,
        grid=(t // TR_RANK,),
        in_specs=[pl.BlockSpec((TOP_K, TR_RANK), lambda i: (0, i))],
        out_specs=[pl.BlockSpec((TOP_K, TR_RANK), lambda i: (0, i)),
                   pl.BlockSpec((N_EXPERTS, 128), lambda i: (0, 0))],
        out_shape=[jax.ShapeDtypeStruct((TOP_K, t), I32), jax.ShapeDtypeStruct((N_EXPERTS, 128), I32)],
        scratch_shapes=[pltpu.VMEM((N_EXPERTS, 128), F32)],
        compiler_params=_cparams("arbitrary"),
        name="moe_rank",
    )(idx_all)


def _dest_kernel(start_ref, idx_ref, rank_ref, dest_ref):
    idx = idx_ref[...]
    base = jnp.zeros(idx.shape, I32)
    for e in range(N_EXPERTS):
        base = jnp.where(idx == e, start_ref[e], base)
    dest_ref[...] = base + rank_ref[...]


def _dests(pad_start, idx_all, rank_all):
    t = idx_all.shape[1]
    spec = pl.BlockSpec((TOP_K, TR_RANK), lambda i, s: (0, i))
    return pl.pallas_call(
        _dest_kernel,
        grid_spec=pltpu.PrefetchScalarGridSpec(
            num_scalar_prefetch=1, grid=(t // TR_RANK,), in_specs=[spec, spec], out_specs=spec),
        out_shape=jax.ShapeDtypeStruct((TOP_K, t), I32),
        compiler_params=_cparams("parallel"),
        name="moe_dest",
    )(pad_start, idx_all, rank_all)


def _row_copy(src_ref, src_row, dst_ref, dst_row, sem):
    return pltpu.make_async_copy(src_ref.at[pl.ds(src_row, 1)], dst_ref.at[pl.ds(dst_row, 1)], sem)


def _dispatch_kernel(cnt_ref, start_ref, hp_ref, hs_ref, dest_hbm, xs_out, dest_smem, zero_ref,
                     sem_idx, sem_rows, sem_fill, *, n_prompt):
    i = pl.program_id(0)
    n = TM_MOE * TOP_K
    cp = pltpu.make_async_copy(dest_hbm.at[pl.ds(i * n, n)], dest_smem, sem_idx)
    cp.start()

    @pl.when(i == 0)
    def _():
        zero_ref[...] = jnp.zeros(zero_ref.shape, U32)
        for e in range(N_EXPERTS):
            n_pad = (cnt_ref[e] + MOE_BLOCK - 1) // MOE_BLOCK * MOE_BLOCK - cnt_ref[e]
            first = start_ref[e] + cnt_ref[e]

            def fill(r, c):
                _row_copy(zero_ref, 0, xs_out, first + r, sem_fill).start()
                return c

            lax.fori_loop(0, n_pad, fill, 0)

            def drain_fill(r, c):
                _row_copy(zero_ref, 0, xs_out, 0, sem_fill).wait()
                return c

            lax.fori_loop(0, n_pad, drain_fill, 0)

        last = N_EXPERTS - 1
        n_used = (start_ref[last] + cnt_ref[last] + MOE_BLOCK - 1) // MOE_BLOCK
        n_blocks = xs_out.shape[0] // MOE_BLOCK

        def block_copy(b):
            return pltpu.make_async_copy(zero_ref, xs_out.at[pl.ds(b * MOE_BLOCK, MOE_BLOCK)], sem_fill)

        def fill_block(b, c):
            block_copy(b).start()
            return c

        lax.fori_loop(n_used, n_blocks, fill_block, 0)

        def drain_block(b, c):
            block_copy(b).wait()
            return c

        lax.fori_loop(n_used, n_blocks, drain_block, 0)

    cp.wait()

    def scatter_from(src_ref):
        def issue(t, c):
            for k in range(TOP_K):
                _row_copy(src_ref, t, xs_out, dest_smem[t * TOP_K + k], sem_rows).start()
            return c

        lax.fori_loop(0, TM_MOE, issue, 0, unroll=2)

    @pl.when(i < n_prompt)
    def _():
        scatter_from(hp_ref)

    @pl.when(i >= n_prompt)
    def _():
        scatter_from(hs_ref)

    tile_rows = xs_out.at[pl.ds(0, n)]
    pltpu.make_async_copy(tile_rows, tile_rows, sem_rows).wait()


def _dispatch(counts, pad_start, h2p_p, h2p_s, dest_flat, cap):
    n_prompt = h2p_p.shape[0] // TM_MOE
    n_sample = h2p_s.shape[0] // TM_MOE
    half = D_MODEL // 2
    hbm = pl.BlockSpec(memory_space=pl.ANY)
    return pl.pallas_call(
        functools.partial(_dispatch_kernel, n_prompt=n_prompt),
        grid_spec=pltpu.PrefetchScalarGridSpec(
            num_scalar_prefetch=2, grid=(n_prompt + n_sample,),
            in_specs=[pl.BlockSpec((TM_MOE, half), lambda i, c, s: (jnp.minimum(i, n_prompt - 1), 0)),
                      pl.BlockSpec((TM_MOE, half), lambda i, c, s: (jnp.maximum(i - n_prompt, 0), 0)),
                      hbm],
            out_specs=hbm,
            scratch_shapes=[pltpu.SMEM((TM_MOE * TOP_K,), I32), pltpu.VMEM((MOE_BLOCK, half), U32),
                            pltpu.SemaphoreType.DMA, pltpu.SemaphoreType.DMA, pltpu.SemaphoreType.DMA]),
        out_shape=jax.ShapeDtypeStruct((cap, half), U32),
        compiler_params=_cparams("arbitrary"),
        name="moe_dispatch",
    )(counts, pad_start, h2p_p, h2p_s, dest_flat)


def _expert_kernel(be_ref, act_ref, new_ref, x_ref, wg_ref, bg_ref, wu_ref, bu_ref, wd_ref, bd_ref, o_ref,
                   wgb_ref, wub_ref, wdb_ref):
    b = pl.program_id(0)

    @pl.when(new_ref[b] == 1)
    def _():
        wgb_ref[...] = wg_ref[0].astype(BF16)
        wub_ref[...] = wu_ref[0].astype(BF16)
        wdb_ref[...] = wd_ref[0].astype(BF16)

    @pl.when(act_ref[b] == 1)
    def _():
        p = x_ref[...]
        lo = pltpu.bitcast(p << 16, F32)
        hi = pltpu.bitcast(p & jnp.uint32(0xFFFF0000), F32)
        x = jnp.concatenate([lo, hi], axis=-1).astype(BF16)
        g = jnp.dot(x, wgb_ref[...], preferred_element_type=F32) + bg_ref[0]
        u = jnp.dot(x, wub_ref[...], preferred_element_type=F32) + bu_ref[0]
        g = jnp.minimum(g, SWIGLU_LIMIT)
        u = jnp.clip(u, -SWIGLU_LIMIT, SWIGLU_LIMIT)
        a = (u + 1.0) * (g * jax.nn.sigmoid(SWIGLU_ALPHA * g))
        o_ref[...] = jnp.dot(a.astype(BF16), wdb_ref[...], preferred_element_type=F32) + bd_ref[0]

    @pl.when(act_ref[b] == 0)
    def _():
        o_ref[...] = jnp.zeros(o_ref.shape, F32)


def _experts(block_expert, block_active, block_new, xs, wg, bg, wu, bu, wd, bd):
    cap = xs.shape[0]
    n_blocks = cap // MOE_BLOCK
    wspec = pl.BlockSpec((1, D_MODEL, D_MODEL), lambda b, be, act, new: (be[b], 0, 0))
    bspec = pl.BlockSpec((1, 1, D_MODEL), lambda b, be, act, new: (be[b], 0, 0))
    return pl.pallas_call(
        _expert_kernel,
        grid_spec=pltpu.PrefetchScalarGridSpec(
            num_scalar_prefetch=3, grid=(n_blocks,),
            in_specs=[pl.BlockSpec((MOE_BLOCK, D_MODEL // 2), lambda b, be, act, new: (b * act[b], 0)),
                      wspec, bspec, wspec, bspec, wspec, bspec],
            out_specs=pl.BlockSpec((MOE_BLOCK, D_MODEL), lambda b, be, act, new: (b, 0)),
            scratch_shapes=[pltpu.VMEM((D_MODEL, D_MODEL), BF16)] * 3),
        out_shape=jax.ShapeDtypeStruct((cap, D_MODEL), F32),
        compiler_params=_cparams("arbitrary"),
        name="moe_experts",
    )(block_expert, block_active, block_new, xs, wg, bg, wu, bu, wd, bd)


def _combine_kernel(hp_ref, hs_ref, gate_ref, dest_hbm, yb_hbm, g_ref, b_ref, yp_ref, ys_ref,
                    dest_smem, rows_ref, sem_idx, sem_rows, *, n_prompt):
    i = pl.program_id(0)
    n = TM_MOE * TOP_K
    slot = i % 2

    def gather_tile(tile, s):
        cp = pltpu.make_async_copy(dest_hbm.at[pl.ds(tile * n, n)], dest_smem.at[pl.ds(s * n, n)], sem_idx)
        cp.start()
        cp.wait()

        def issue(t, c):
            for k in range(TOP_K):
                _row_copy(yb_hbm, dest_smem[s * n + t * TOP_K + k], rows_ref.at[s, k], t, sem_rows.at[s]).start()
            return c

        lax.fori_loop(0, TM_MOE, issue, 0, unroll=2)

    @pl.when(i == 0)
    def _():
        gather_tile(0, 0)

    @pl.when(i + 1 < pl.num_programs(0))
    def _():
        gather_tile(i + 1, 1 - slot)

    pltpu.make_async_copy(rows_ref.at[slot], rows_ref.at[slot], sem_rows.at[slot]).wait()

    gate = gate_ref[...]
    ff = gate[:, 0:1] * rows_ref[slot, 0]
    for k in range(1, TOP_K):
        ff = ff + gate[:, k:k + 1] * rows_ref[slot, k]

    @pl.when(i < n_prompt)
    def _():
        yp_ref[...] = _layer_norm(DEEPNORM_ALPHA * hp_ref[...] + ff, g_ref[...], b_ref[...])

    @pl.when(i >= n_prompt)
    def _():
        ys_ref[...] = _layer_norm(DEEPNORM_ALPHA * hs_ref[...] + ff, g_ref[...], b_ref[...])


def _combine(h2_p, h2_s, gates, dest_flat, yb, ln_g, ln_b):
    n_prompt = h2_p.shape[0] // TM_MOE
    n_sample = h2_s.shape[0] // TM_MOE
    pmap = lambda i: (jnp.minimum(i, n_prompt - 1), 0)
    smap = lambda i: (jnp.maximum(i - n_prompt, 0), 0)
    vec = pl.BlockSpec((1, D_MODEL), lambda i: (0, 0))
    return pl.pallas_call(
        functools.partial(_combine_kernel, n_prompt=n_prompt),
        grid=(n_prompt + n_sample,),
        in_specs=[pl.BlockSpec((TM_MOE, D_MODEL), pmap), pl.BlockSpec((TM_MOE, D_MODEL), smap),
                  pl.BlockSpec((TM_MOE, TOP_K), lambda i: (i, 0)),
                  pl.BlockSpec(memory_space=pl.ANY), pl.BlockSpec(memory_space=pl.ANY), vec, vec],
        out_specs=[pl.BlockSpec((TM_MOE, D_MODEL), pmap), pl.BlockSpec((TM_MOE, D_MODEL), smap)],
        out_shape=[jax.ShapeDtypeStruct(h2_p.shape, F32), jax.ShapeDtypeStruct(h2_s.shape, F32)],
        scratch_shapes=[pltpu.SMEM((2 * TM_MOE * TOP_K,), I32),
                        pltpu.VMEM((2, TOP_K, TM_MOE, D_MODEL), F32),
                        pltpu.SemaphoreType.DMA, pltpu.SemaphoreType.DMA((2,))],
        compiler_params=_cparams("arbitrary"),
        name="moe_combine",
    )(h2_p, h2_s, gates, dest_flat, yb, ln_g, ln_b)


def _tokens_major(a):
    n, k, tm = a.shape
    return jnp.transpose(a, (1, 0, 2)).reshape(k, n * tm)


def kernel(x_prompt, x_sample, mem_prompt, cache_diff_k, cache_diff_v, state_pool, cache_mem_k, cache_mem_v, w_in, lambda_q1, lambda_k1, lambda_q2, lambda_k2, subln_g, pool_w, pool_scale, w_out, ln1_g, ln1_b, xa_wq, xa_wk, xa_wv, xa_wo, ln2_g, ln2_b, router_w, router_b, moe_w_gate, moe_b_gate, moe_w_up, moe_b_up, moe_w_down, moe_b_down, ln3_g, ln3_b):
    assert w_in.shape[0] == DEPTH == 1
    batch, seq, d = x_prompt.shape
    dec_batch, dec_seq, _ = x_sample.shape
    past_len = cache_diff_k.shape[2]
    lambda_init = 0.8 - 0.6 * math.exp(-0.3 * 0)
    tp, ts = batch * seq, dec_batch * dec_seq

    vec = lambda a: a[0].reshape(1, -1)
    rw_t = router_w[0].T
    rw_hi = rw_t.astype(BF16)
    wts = dict(
        pool_w=pool_w[0].astype(BF16), pool_scale=vec(pool_scale), w_out=w_out[0].astype(BF16),
        ln1_g=vec(ln1_g), ln1_b=vec(ln1_b), xa_wq=xa_wq[0].astype(BF16), xa_wo=xa_wo[0].astype(BF16),
        ln2_g=vec(ln2_g), ln2_b=vec(ln2_b), rw_hi=rw_hi, rw_lo=(rw_t - rw_hi.astype(F32)).astype(BF16),
        router_b=router_b[0].reshape(N_EXPERTS, 1))
    w_in_bf = w_in[0].astype(BF16)
    lam_vecs = [vec(lambda_q1), vec(lambda_k1), vec(lambda_q2), vec(lambda_k2)]
    g_sub = vec(subln_g)

    xp2 = x_prompt.reshape(tp, d)
    qp, kp, vp, up, kpb, vpb = _inproj(xp2, w_in_bf, TM_PROJ)
    dp = _diff_prompt(qp, kpb, vpb, lam_vecs, g_sub, batch, seq, lambda_init)
    mk, mv, mkb, mvb = _memkv(mem_prompt.reshape(batch * N_MEM, d), xa_wk[0].astype(BF16),
                              xa_wv[0].astype(BF16), TM_PROJ)
    per16 = TM_POST // HIST_ROWS
    h2_p, h2p_p, idx_p, gate_p = _post(
        dp, up, up, lambda i: (jnp.maximum(i * per16 - 1, 0), 0), xp2, mkb, mvb, wts,
        tm=TM_POST, seq=seq, pos0=0, zero_first_hist=True)

    xs2 = x_sample.reshape(ts, d)
    qs, ks, vs, us, ksb, vsb = _inproj(xs2, w_in_bf, TM_PROJ)
    ds = _diff_sample(qs, cache_diff_k, cache_diff_v, ksb, vsb, lam_vecs, g_sub, dec_batch, dec_seq, lambda_init)
    hist_s = jnp.pad(state_pool[0], ((0, 0), (1, 0), (0, 0))).reshape(dec_batch * HIST_ROWS, POOL_WIDTH)
    h2_s, h2p_s, idx_s, gate_s = _post(
        ds, us, hist_s, lambda i: (i, 0), xs2,
        cache_mem_k[0].reshape(dec_batch * N_MEM, d).astype(BF16),
        cache_mem_v[0].reshape(dec_batch * N_MEM, d).astype(BF16), wts,
        tm=dec_seq, seq=dec_seq, pos0=past_len, zero_first_hist=False)

    t_all = tp + ts
    idx_all = jnp.concatenate([_tokens_major(idx_p), _tokens_major(idx_s)], axis=1)
    gate_all = jnp.concatenate([_tokens_major(gate_p), _tokens_major(gate_s)], axis=1)
    rank_all, counts = _ranks(idx_all)
    counts = counts[:, 0]
    padded = (counts + MOE_BLOCK - 1) // MOE_BLOCK * MOE_BLOCK
    pad_end = jnp.cumsum(padded)
    pad_start = (pad_end - padded).astype(I32)
    n_blocks = t_all * TOP_K // MOE_BLOCK + N_EXPERTS
    block_start = jnp.arange(n_blocks, dtype=I32) * MOE_BLOCK
    block_expert = jnp.minimum(jnp.sum(block_start[:, None] >= pad_end[None, :], axis=1), N_EXPERTS - 1).astype(I32)
    block_active = (block_start < pad_end[-1]).astype(I32)
    block_new = jnp.concatenate([jnp.ones((1,), I32), (block_expert[1:] != block_expert[:-1]).astype(I32)])
    dest_flat = _dests(pad_start, idx_all, rank_all).T.reshape(-1)
    xs = _dispatch(counts, pad_start, h2p_p, h2p_s, dest_flat, n_blocks * MOE_BLOCK)
    yb = _experts(block_expert, block_active, block_new, xs,
                  moe_w_gate[0], moe_b_gate[0].reshape(N_EXPERTS, 1, D_MODEL),
                  moe_w_up[0], moe_b_up[0].reshape(N_EXPERTS, 1, D_MODEL),
                  moe_w_down[0], moe_b_down[0].reshape(N_EXPERTS, 1, D_MODEL))
    y_p, y_s = _combine(h2_p, h2_s, gate_all.T, dest_flat, yb, vec(ln3_g), vec(ln3_b))

    heads = (N_DIFF_HEADS, DIFF_V_DIM)
    xa = (N_XA_HEADS, XA_HEAD_DIM)
    up3 = up.reshape(batch, seq, POOL_WIDTH)
    us3 = us.reshape(dec_batch, dec_seq, POOL_WIDTH)
    pool_s = jnp.concatenate([state_pool[0].astype(F32), us3], axis=1)[:, -POOL_HIST:]
    return (y_p.reshape(batch, seq, d), y_s.reshape(dec_batch, dec_seq, d),
            kp.reshape(1, batch, seq, *heads), vp.reshape(1, batch, seq, *heads),
            up3[:, seq - POOL_HIST:][None],
            mk.reshape(1, batch, N_MEM, *xa), mv.reshape(1, batch, N_MEM, *xa),
            ks.reshape(1, dec_batch, dec_seq, *heads), vs.reshape(1, dec_batch, dec_seq, *heads),
            pool_s[None])
```

```python
import functools
import math

import jax
import jax.numpy as jnp
from jax import lax
from jax.experimental import pallas as pl
from jax.experimental.pallas import tpu as pltpu

F32 = jnp.float32
BF16 = jnp.bfloat16
I32 = jnp.int32
U32 = jnp.uint32

D_MODEL = 1024
CHUNK = 64
N_DIFF_HEADS = 4
DIFF_HEAD_DIM = 64
DIFF_V_DIM = 2 * DIFF_HEAD_DIM
DIFF_WIDTH = N_DIFF_HEADS * DIFF_V_DIM
POOL_WIDTH = D_MODEL - DIFF_WIDTH
POOL_WINDOWS = (2, 4, 8, 16)
POOL_GROUP_DIM = POOL_WIDTH // len(POOL_WINDOWS)
POOL_HIST = max(POOL_WINDOWS) - 1
HIST_ROWS = POOL_HIST + 1
IN_WIDTH = 3 * DIFF_WIDTH + POOL_WIDTH
N_MEM = 256
N_XA_HEADS = 4
XA_HEAD_DIM = D_MODEL // N_XA_HEADS
N_EXPERTS = 32
TOP_K = 4
SWIGLU_LIMIT = 7.0
SWIGLU_ALPHA = 1.702
MOE_BLOCK = 256
LN_EPS = 1e-5
RMS_EPS = 1e-5
DEPTH = 1
DEEPNORM_ALPHA = (2.0 * DEPTH) ** 0.25
NEG_BIG = -1e30
LOG2_E = math.log2(math.e)

VMEM_LIMIT = 56 * 1024 * 1024

TM_PROJ = 512
TQ_ATT = 256
TK_ATT = 512
TK_PAST = 1024
TM_POST = 512
TR_RANK = 512
TM_MOE = 256


def _cparams(*sem):
    return pltpu.CompilerParams(dimension_semantics=sem, vmem_limit_bytes=VMEM_LIMIT)


def _layer_norm(x, g, b):
    mu = jnp.mean(x, axis=-1, keepdims=True)
    xc = x - mu
    var = jnp.mean(xc * xc, axis=-1, keepdims=True)
    return xc * lax.rsqrt(var + LN_EPS) * g + b


def _inproj_kernel(x_ref, w_ref, q_ref, k_ref, v_ref, u_ref, kb_ref, vb_ref):
    x = x_ref[...].astype(BF16)
    w = DIFF_WIDTH
    q = jnp.dot(x, w_ref[:, 0:w], preferred_element_type=F32)
    q_ref[...] = (q * (DIFF_HEAD_DIM ** -0.5 * LOG2_E)).astype(BF16)
    k = jnp.dot(x, w_ref[:, w:2 * w], preferred_element_type=F32)
    kb_ref[...] = k.astype(BF16)
    v = jnp.dot(x, w_ref[:, 2 * w:3 * w], preferred_element_type=F32)
    vb_ref[...] = v.astype(BF16)
    for h in range(N_DIFF_HEADS):
        cs = slice(h * DIFF_V_DIM, (h + 1) * DIFF_V_DIM)
        k_ref[:, h, :] = k[:, cs]
        v_ref[:, h, :] = v[:, cs]
    u_ref[...] = jnp.dot(x, w_ref[:, 3 * w:], preferred_element_type=F32)


def _inproj(x2d, w_in_bf, tm):
    t = x2d.shape[0]
    w = DIFF_WIDTH
    row = lambda i: (i, 0)
    out_spec = pl.BlockSpec((tm, w), row)
    head_spec = pl.BlockSpec((tm, N_DIFF_HEADS, DIFF_V_DIM), lambda i: (i, 0, 0))
    head_shape = jax.ShapeDtypeStruct((t, N_DIFF_HEADS, DIFF_V_DIM), F32)
    return pl.pallas_call(
        _inproj_kernel,
        grid=(t // tm,),
        in_specs=[pl.BlockSpec((tm, D_MODEL), row),
                  pl.BlockSpec((D_MODEL, IN_WIDTH), lambda i: (0, 0))],
        out_specs=[out_spec, head_spec, head_spec, out_spec, out_spec, out_spec],
        out_shape=[jax.ShapeDtypeStruct((t, w), BF16), head_shape, head_shape,
                   jax.ShapeDtypeStruct((t, w), F32),
                   jax.ShapeDtypeStruct((t, w), BF16), jax.ShapeDtypeStruct((t, w), BF16)],
        compiler_params=_cparams("parallel"),
        name="inproj",
    )(x2d, w_in_bf)


def _memkv_kernel(x_ref, wk_ref, wv_ref, k_ref, v_ref, kb_ref, vb_ref):
    x = x_ref[...].astype(BF16)
    k = jnp.dot(x, wk_ref[...], preferred_element_type=F32)
    kb_ref[...] = k.astype(BF16)
    v = jnp.dot(x, wv_ref[...], preferred_element_type=F32)
    vb_ref[...] = v.astype(BF16)
    for h in range(N_XA_HEADS):
        cs = slice(h * XA_HEAD_DIM, (h + 1) * XA_HEAD_DIM)
        k_ref[:, h, :] = k[:, cs]
        v_ref[:, h, :] = v[:, cs]


def _memkv(mem2d, wk_bf, wv_bf, tm):
    t = mem2d.shape[0]
    row = lambda i: (i, 0)
    full = lambda i: (0, 0)
    spec = pl.BlockSpec((tm, D_MODEL), row)
    head_spec = pl.BlockSpec((tm, N_XA_HEADS, XA_HEAD_DIM), lambda i: (i, 0, 0))
    head_shape = jax.ShapeDtypeStruct((t, N_XA_HEADS, XA_HEAD_DIM), F32)
    return pl.pallas_call(
        _memkv_kernel,
        grid=(t // tm,),
        in_specs=[spec, pl.BlockSpec((D_MODEL, D_MODEL), full), pl.BlockSpec((D_MODEL, D_MODEL), full)],
        out_specs=[head_spec, head_spec, spec, spec],
        out_shape=[head_shape, head_shape,
                   jax.ShapeDtypeStruct((t, D_MODEL), BF16), jax.ShapeDtypeStruct((t, D_MODEL), BF16)],
        compiler_params=_cparams("parallel"),
        name="memkv",
    )(mem2d, wk_bf, wv_bf)


def _diff_lambda(lq1_ref, lk1_ref, lq2_ref, lk2_ref, lambda_init):
    s1 = jnp.sum(lq1_ref[...] * lk1_ref[...], axis=-1, keepdims=True)
    s2 = jnp.sum(lq2_ref[...] * lk2_ref[...], axis=-1, keepdims=True)
    return jnp.exp(s1) - jnp.exp(s2) + lambda_init


def _stack_maps(qh):
    lane = lax.broadcasted_iota(I32, qh.shape, 1)
    zero = jnp.zeros_like(qh)
    return jnp.concatenate([jnp.where(lane < DIFF_HEAD_DIM, qh, zero),
                            jnp.where(lane >= DIFF_HEAD_DIM, qh, zero)], axis=0)


def _lanes(x, n):
    w = x.shape[1]
    return x[:, :n] if n <= w else jnp.tile(x, (1, n // w))


def _softmax_step(qq, kh, vh, m_ref, acc_ref, mask):
    s = lax.dot_general(qq, kh, (((1,), (1,)), ((), ())), preferred_element_type=F32)
    if mask is not None:
        s = jnp.where(mask, s, NEG_BIG)
    m_old = m_ref[...]
    m_new = jnp.maximum(m_old, jnp.max(s, axis=-1, keepdims=True))
    p = jnp.exp2(s - _lanes(m_new, s.shape[1])).astype(BF16)
    ones_col = (lax.broadcasted_iota(I32, vh.shape, 1) == 0).astype(BF16)
    pv = jnp.dot(p, jnp.concatenate([vh, ones_col], axis=1), preferred_element_type=F32)
    a = jnp.exp2(m_old - m_new)
    acc_ref[...] = _lanes(a, pv.shape[1]) * acc_ref[...] + pv
    m_ref[...] = m_new


def _diff_finish(acc, lam, g, lambda_init, tq):
    hd = DIFF_V_DIM
    o = acc[:tq, :hd] / acc[:tq, hd:hd + 1] - lam * (acc[tq:, :hd] / acc[tq:, hd:hd + 1])
    o = o * lax.rsqrt(jnp.mean(o * o, axis=-1, keepdims=True) + RMS_EPS)
    return o * g * (1.0 - lambda_init)


def _diff_prompt_kernel(q_ref, k_ref, v_ref, lq1_ref, lk1_ref, lq2_ref, lk2_ref, g_ref, o_ref, m_ref, acc_ref,
                        *, lambda_init):
    i = pl.program_id(1)
    tq, tk, hd = TQ_ATT, TK_ATT, DIFF_V_DIM
    n_full = (i * tq) // tk
    qrow = lax.broadcasted_iota(I32, (2 * tq, tk), 0)
    qrow = jnp.where(qrow >= tq, qrow - tq, qrow) + i * tq
    kcol = lax.broadcasted_iota(I32, (2 * tq, tk), 1) + n_full * tk
    last_mask = (kcol // CHUNK) <= (qrow // CHUNK)
    m_ref[...] = jnp.full(m_ref.shape, NEG_BIG, F32)
    acc_ref[...] = jnp.zeros(acc_ref.shape, F32)

    def step(j, mask):
        r0 = pl.multiple_of(j * tk, tk)
        for h in range(N_DIFF_HEADS):
            cs = slice(h * hd, (h + 1) * hd)
            _softmax_step(_stack_maps(q_ref[:, cs]), k_ref[pl.ds(r0, tk), cs], v_ref[pl.ds(r0, tk), cs],
                          m_ref.at[h], acc_ref.at[h], mask)

    def body(j, c):
        step(j, None)
        return c

    lax.fori_loop(0, n_full, body, 0)
    step(n_full, last_mask)
    lam = _diff_lambda(lq1_ref, lk1_ref, lq2_ref, lk2_ref, lambda_init)
    for h in range(N_DIFF_HEADS):
        o_ref[:, h * hd:(h + 1) * hd] = _diff_finish(acc_ref[h], lam, g_ref[...], lambda_init,
                                                     tq).astype(o_ref.dtype)


def _diff_prompt(qb, kb, vb, lam_vecs, subln_g, batch, seq, lambda_init):
    assert TQ_ATT % CHUNK == 0 and TK_ATT % TQ_ATT == 0 and seq % TK_ATT == 0
    nq = seq // TQ_ATT
    w = DIFF_WIDTH
    small = lambda b, i: (0, 0)
    return pl.pallas_call(
        functools.partial(_diff_prompt_kernel, lambda_init=lambda_init),
        grid=(batch, nq),
        in_specs=[pl.BlockSpec((TQ_ATT, w), lambda b, i: (b * nq + i, 0)),
                  pl.BlockSpec((seq, w), lambda b, i: (b, 0)),
                  pl.BlockSpec((seq, w), lambda b, i: (b, 0))]
                 + [pl.BlockSpec((1, DIFF_HEAD_DIM), small)] * 4
                 + [pl.BlockSpec((1, DIFF_V_DIM), small)],
        out_specs=pl.BlockSpec((TQ_ATT, w), lambda b, i: (b * nq + i, 0)),
        out_shape=jax.ShapeDtypeStruct((batch * seq, w), BF16),
        scratch_shapes=[pltpu.VMEM((N_DIFF_HEADS, 2 * TQ_ATT, 128), F32),
                        pltpu.VMEM((N_DIFF_HEADS, 2 * TQ_ATT, 2 * DIFF_V_DIM), F32)],
        compiler_params=_cparams("parallel", "arbitrary"),
        name="diff_attn_prompt",
    )(qb, kb, vb, *lam_vecs, subln_g)


def _diff_sample_kernel(q_ref, pk_ref, pv_ref, nk_ref, nv_ref, lq1_ref, lk1_ref, lq2_ref, lk2_ref, g_ref,
                        o_ref, m_ref, acc_ref, *, lambda_init, n_past):
    j = pl.program_id(1)
    tq, hd = q_ref.shape[0], DIFF_V_DIM

    @pl.when(j == 0)
    def _():
        m_ref[...] = jnp.full(m_ref.shape, NEG_BIG, F32)
        acc_ref[...] = jnp.zeros(acc_ref.shape, F32)

    def update(k_of, v_of):
        for h in range(N_DIFF_HEADS):
            qq = _stack_maps(q_ref[:, h * hd:(h + 1) * hd])
            _softmax_step(qq, k_of(h), v_of(h), m_ref.at[h], acc_ref.at[h], None)

    @pl.when(j < n_past)
    def _():
        update(lambda h: pk_ref[0, 0, :, h, :].astype(BF16), lambda h: pv_ref[0, 0, :, h, :].astype(BF16))

    @pl.when(j == n_past)
    def _():
        update(lambda h: nk_ref[:, h * hd:(h + 1) * hd], lambda h: nv_ref[:, h * hd:(h + 1) * hd])
        lam = _diff_lambda(lq1_ref, lk1_ref, lq2_ref, lk2_ref, lambda_init)
        for h in range(N_DIFF_HEADS):
            o_ref[:, h * hd:(h + 1) * hd] = _diff_finish(acc_ref[h], lam, g_ref[...], lambda_init,
                                                         tq).astype(o_ref.dtype)


def _diff_sample(qb, past_k, past_v, kb, vb, lam_vecs, subln_g, batch, seq, lambda_init):
    past_len = past_k.shape[2]
    assert past_len % TK_PAST == 0
    n_past = past_len // TK_PAST
    w = DIFF_WIDTH
    small = lambda b, j: (0, 0)
    past_spec = pl.BlockSpec((1, 1, TK_PAST, N_DIFF_HEADS, DIFF_V_DIM),
                             lambda b, j: (0, b, jnp.minimum(j, n_past - 1), 0, 0))
    new_spec = pl.BlockSpec((seq, w), lambda b, j: (b, 0))
    return pl.pallas_call(
        functools.partial(_diff_sample_kernel, lambda_init=lambda_init, n_past=n_past),
        grid=(batch, n_past + 1),
        in_specs=[new_spec, past_spec, past_spec, new_spec, new_spec]
                 + [pl.BlockSpec((1, DIFF_HEAD_DIM), small)] * 4
                 + [pl.BlockSpec((1, DIFF_V_DIM), small)],
        out_specs=new_spec,
        out_shape=jax.ShapeDtypeStruct((batch * seq, w), BF16),
        scratch_shapes=[pltpu.VMEM((N_DIFF_HEADS, 2 * seq, 128), F32),
                        pltpu.VMEM((N_DIFF_HEADS, 2 * seq, 2 * DIFF_V_DIM), F32)],
        compiler_params=_cparams("parallel", "arbitrary"),
        name="diff_attn_sample",
    )(qb, past_k, past_v, kb, vb, *lam_vecs, subln_g)


def _post_kernel(d_ref, u_ref, hist_ref, x_ref, pw_ref, ps_ref, wo1_ref, g1_ref, b1_ref, wq_ref, mk_ref,
                 mv_ref, wo2_ref, g2_ref, b2_ref, rwh_ref, rwl_ref, rb_ref,
                 h2_ref, h2p_ref, idx_ref, gate_ref, full_ref, *, tm, seq, pos0, zero_first_hist):
    i = pl.program_id(0)
    row0 = (i * tm) % seq

    hist = hist_ref[...]
    if zero_first_hist:
        hist = jnp.where(row0 == 0, jnp.zeros_like(hist), hist)
    full_ref[0:HIST_ROWS, :] = hist
    u = u_ref[...]
    full_ref[HIST_ROWS:, :] = u
    pos = pos0 + row0 + lax.broadcasted_iota(I32, (tm, 1), 0)
    pooled = []
    for gi, w in enumerate(POOL_WINDOWS):
        cs = slice(gi * POOL_GROUP_DIM, (gi + 1) * POOL_GROUP_DIM)
        acc = u[:, cs]
        for back in range(1, w):
            acc = acc + full_ref[HIST_ROWS - back:HIST_ROWS - back + tm, cs]
        cnt = jnp.minimum(w, pos + 1).astype(F32)
        m = acc / cnt - u[:, cs]
        y = jnp.dot(m.astype(BF16), pw_ref[gi], preferred_element_type=F32)
        pooled.append((y * ps_ref[:, cs]).astype(BF16))
    mixed_in = jnp.concatenate([d_ref[...]] + pooled, axis=-1)

    mix = jnp.dot(mixed_in, wo1_ref[...], preferred_element_type=F32)
    h1 = _layer_norm(DEEPNORM_ALPHA * x_ref[...] + mix, g1_ref[...], b1_ref[...])

    q = jnp.dot(h1.astype(BF16), wq_ref[...], preferred_element_type=F32)
    qb = (q * (XA_HEAD_DIM ** -0.5)).astype(BF16)
    heads = []
    for h in range(N_XA_HEADS):
        cs = slice(h * XA_HEAD_DIM, (h + 1) * XA_HEAD_DIM)
        s = lax.dot_general(qb[:, cs], mk_ref[:, cs], (((1,), (1,)), ((), ())), preferred_element_type=F32)
        e = jnp.exp(s - jnp.max(s, axis=-1, keepdims=True))
        p = e / jnp.sum(e, axis=-1, keepdims=True)
        heads.append(jnp.dot(p.astype(BF16), mv_ref[:, cs], preferred_element_type=F32).astype(BF16))
    ca = jnp.dot(jnp.concatenate(heads, axis=-1), wo2_ref[...], preferred_element_type=F32)
    h2 = _layer_norm(DEEPNORM_ALPHA * h1 + ca, g2_ref[...], b2_ref[...])
    h2_ref[...] = h2

    hb = h2.astype(BF16)
    bits = pltpu.bitcast(hb.astype(F32), U32)
    half = D_MODEL // 2
    h2p_ref[...] = (bits[:, :half] >> 16) | (bits[:, half:] & jnp.uint32(0xFFFF0000))

    hl = (h2 - hb.astype(F32)).astype(BF16)
    dn = (((1,), (1,)), ((), ()))
    logits = (lax.dot_general(rwh_ref[...], hb, dn, preferred_element_type=F32)
              + lax.dot_general(rwh_ref[...], hl, dn, preferred_element_type=F32)
              + lax.dot_general(rwl_ref[...], hb, dn, preferred_element_type=F32)
              + rb_ref[...])
    erow = lax.broadcasted_iota(I32, logits.shape, 0)
    vals, idxs = [], []
    for _ in range(TOP_K):
        mx = jnp.max(logits, axis=0, keepdims=True)
        ix = jnp.min(jnp.where(logits == mx, erow, N_EXPERTS), axis=0, keepdims=True)
        vals.append(mx)
        idxs.append(ix)
        logits = jnp.where(erow == ix, -jnp.inf, logits)
    ex = [jnp.exp(v - vals[0]) for v in vals]
    den = ex[0] + ex[1] + ex[2] + ex[3]
    idx_ref[0] = jnp.concatenate(idxs, axis=0)
    gate_ref[0] = jnp.concatenate([e / den for e in ex], axis=0)


def _post(diff_out, u, hist_arr, hist_map, x2d, mkb, mvb, wts, *, tm, seq, pos0, zero_first_hist):
    t = x2d.shape[0]
    n = t // tm
    assert seq % tm == 0 or tm % seq == 0
    row = lambda i: (i, 0)
    full2 = lambda i: (0, 0)
    mem_map = lambda i: ((i * tm) // seq, 0)
    vec = pl.BlockSpec((1, D_MODEL), full2)
    wspec = pl.BlockSpec((D_MODEL, D_MODEL), full2)
    in_specs = [
        pl.BlockSpec((tm, DIFF_WIDTH), row), pl.BlockSpec((tm, POOL_WIDTH), row),
        pl.BlockSpec((HIST_ROWS, POOL_WIDTH), hist_map), pl.BlockSpec((tm, D_MODEL), row),
        pl.BlockSpec((len(POOL_WINDOWS), POOL_GROUP_DIM, POOL_GROUP_DIM), lambda i: (0, 0, 0)),
        pl.BlockSpec((1, POOL_WIDTH), full2),
        wspec, vec, vec,
        wspec, pl.BlockSpec((N_MEM, D_MODEL), mem_map), pl.BlockSpec((N_MEM, D_MODEL), mem_map),
        wspec, vec, vec,
        pl.BlockSpec((N_EXPERTS, D_MODEL), full2), pl.BlockSpec((N_EXPERTS, D_MODEL), full2),
        pl.BlockSpec((N_EXPERTS, 1), full2),
    ]
    out_specs = [pl.BlockSpec((tm, D_MODEL), row), pl.BlockSpec((tm, D_MODEL // 2), row),
                 pl.BlockSpec((1, TOP_K, tm), lambda i: (i, 0, 0)),
                 pl.BlockSpec((1, TOP_K, tm), lambda i: (i, 0, 0))]
    out_shape = [jax.ShapeDtypeStruct((t, D_MODEL), F32), jax.ShapeDtypeStruct((t, D_MODEL // 2), U32),
                 jax.ShapeDtypeStruct((n, TOP_K, tm), I32), jax.ShapeDtypeStruct((n, TOP_K, tm), F32)]
    return pl.pallas_call(
        functools.partial(_post_kernel, tm=tm, seq=seq, pos0=pos0, zero_first_hist=zero_first_hist),
        grid=(n,),
        in_specs=in_specs,
        out_specs=out_specs,
        out_shape=out_shape,
        scratch_shapes=[pltpu.VMEM((HIST_ROWS + tm, POOL_WIDTH), F32)],
        compiler_params=_cparams("parallel"),
        name="post_attn",
    )(diff_out, u, hist_arr, x2d, wts["pool_w"], wts["pool_scale"], wts["w_out"], wts["ln1_g"], wts["ln1_b"],
      wts["xa_wq"], mkb, mvb, wts["xa_wo"], wts["ln2_g"], wts["ln2_b"], wts["rw_hi"], wts["rw_lo"],
      wts["router_b"])


def _rank_kernel(idx_ref, rank_ref, cnt_ref, carry_ref):
    @pl.when(pl.program_id(0) == 0)
    def _():
        carry_ref[...] = jnp.zeros(carry_ref.shape, F32)

    idx = idx_ref[...]
    tr = idx.shape[1]
    erow = lax.broadcasted_iota(I32, (N_EXPERTS, tr), 0)
    hits = [erow == idx[k:k + 1, :] for k in range(TOP_K)]
    onehot = sum(h.astype(F32) for h in hits)
    earlier = (lax.broadcasted_iota(I32, (tr, tr), 0) < lax.broadcasted_iota(I32, (tr, tr), 1)).astype(BF16)
    before = jnp.dot(onehot.astype(BF16), earlier, preferred_element_type=F32) + carry_ref[:, 0:1]
    ranks = [jnp.sum(jnp.where(h, before, 0.0), axis=0, keepdims=True) for h in hits]
    rank_ref[...] = jnp.concatenate(ranks, axis=0).astype(I32)
    carry_ref[...] = carry_ref[...] + jnp.sum(onehot, axis=1, keepdims=True)
    cnt_ref[...] = carry_ref[...].astype(I32)


def _ranks(idx_all):
    t = idx_all.shape[1]
    assert t % TR_RANK == 0
    return pl.pallas_call(
        _rank_kernel,
        grid=(t // TR_RANK,),
        in_specs=[pl.BlockSpec((TOP_K, TR_RANK), lambda i: (0, i))],
        out_specs=[pl.BlockSpec((TOP_K, TR_RANK), lambda i: (0, i)),
                   pl.BlockSpec((N_EXPERTS, 128), lambda i: (0, 0))],
        out_shape=[jax.ShapeDtypeStruct((TOP_K, t), I32), jax.ShapeDtypeStruct((N_EXPERTS, 128), I32)],
        scratch_shapes=[pltpu.VMEM((N_EXPERTS, 128), F32)],
        compiler_params=_cparams("arbitrary"),
        name="moe_rank",
    )(idx_all)


def _dest_kernel(start_ref, idx_ref, rank_ref, dest_ref):
    idx = idx_ref[...]
    base = jnp.zeros(idx.shape, I32)
    for e in range(N_EXPERTS):
        base = jnp.where(idx == e, start_ref[e], base)
    dest_ref[...] = base + rank_ref[...]


def _dests(pad_start, idx_all, rank_all):
    t = idx_all.shape[1]
    spec = pl.BlockSpec((TOP_K, TR_RANK), lambda i, s: (0, i))
    return pl.pallas_call(
        _dest_kernel,
        grid_spec=pltpu.PrefetchScalarGridSpec(
            num_scalar_prefetch=1, grid=(t // TR_RANK,), in_specs=[spec, spec], out_specs=spec),
        out_shape=jax.ShapeDtypeStruct((TOP_K, t), I32),
        compiler_params=_cparams("parallel"),
        name="moe_dest",
    )(pad_start, idx_all, rank_all)


def _row_copy(src_ref, src_row, dst_ref, dst_row, sem):
    return pltpu.make_async_copy(src_ref.at[pl.ds(src_row, 1)], dst_ref.at[pl.ds(dst_row, 1), 0], sem)


def _dispatch_kernel(cnt_ref, start_ref, hp_ref, hs_ref, dest_hbm, xs_out, dest_smem, zero_ref,
                     sem_idx, sem_rows, sem_fill, *, n_prompt):
    i = pl.program_id(0)
    n = TM_MOE * TOP_K
    cp = pltpu.make_async_copy(dest_hbm.at[pl.ds(i * n, n)], dest_smem, sem_idx)
    cp.start()

    @pl.when(i == 0)
    def _():
        zero_ref[...] = jnp.zeros(zero_ref.shape, U32)
        for e in range(N_EXPERTS):
            n_pad = (cnt_ref[e] + MOE_BLOCK - 1) // MOE_BLOCK * MOE_BLOCK - cnt_ref[e]
            first = start_ref[e] + cnt_ref[e]

            def fill(r, c):
                _row_copy(zero_ref, 0, xs_out, first + r, sem_fill).start()
                return c

            lax.fori_loop(0, n_pad, fill, 0)

            def drain_fill(r, c):
                _row_copy(zero_ref, 0, xs_out, 0, sem_fill).wait()
                return c

            lax.fori_loop(0, n_pad, drain_fill, 0)

        last = N_EXPERTS - 1
        n_used = (start_ref[last] + cnt_ref[last] + MOE_BLOCK - 1) // MOE_BLOCK
        n_blocks = xs_out.shape[0] // MOE_BLOCK

        def block_copy(b):
            return pltpu.make_async_copy(zero_ref, xs_out.at[pl.ds(b * MOE_BLOCK, MOE_BLOCK), 0], sem_fill)

        def fill_block(b, c):
            block_copy(b).start()
            return c

        lax.fori_loop(n_used, n_blocks, fill_block, 0)

        def drain_block(b, c):
            block_copy(b).wait()
            return c

        lax.fori_loop(n_used, n_blocks, drain_block, 0)

    cp.wait()

    def scatter_from(src_ref):
        def issue(t, c):
            for k in range(TOP_K):
                _row_copy(src_ref, t, xs_out, dest_smem[t * TOP_K + k], sem_rows).start()
            return c

        lax.fori_loop(0, TM_MOE, issue, 0, unroll=2)

    @pl.when(i < n_prompt)
    def _():
        scatter_from(hp_ref)

    @pl.when(i >= n_prompt)
    def _():
        scatter_from(hs_ref)

    tile_rows = xs_out.at[pl.ds(0, n)]
    pltpu.make_async_copy(tile_rows, tile_rows, sem_rows).wait()


def _dispatch(counts, pad_start, h2p_p, h2p_s, dest_flat, cap):
    n_prompt = h2p_p.shape[0] // TM_MOE
    n_sample = h2p_s.shape[0] // TM_MOE
    half = D_MODEL // 2
    hbm = pl.BlockSpec(memory_space=pl.ANY)
    return pl.pallas_call(
        functools.partial(_dispatch_kernel, n_prompt=n_prompt),
        grid_spec=pltpu.PrefetchScalarGridSpec(
            num_scalar_prefetch=2, grid=(n_prompt + n_sample,),
            in_specs=[pl.BlockSpec((TM_MOE, half), lambda i, c, s: (jnp.minimum(i, n_prompt - 1), 0)),
                      pl.BlockSpec((TM_MOE, half), lambda i, c, s: (jnp.maximum(i - n_prompt, 0), 0)),
                      hbm],
            out_specs=hbm,
            scratch_shapes=[pltpu.SMEM((TM_MOE * TOP_K,), I32), pltpu.VMEM((MOE_BLOCK, half), U32),
                            pltpu.SemaphoreType.DMA, pltpu.SemaphoreType.DMA, pltpu.SemaphoreType.DMA]),
        out_shape=jax.ShapeDtypeStruct((cap, 1, half), U32),
        compiler_params=_cparams("arbitrary"),
        name="moe_dispatch",
    )(counts, pad_start, h2p_p, h2p_s, dest_flat)


def _expert_kernel(be_ref, act_ref, new_ref, xs_hbm, wg_ref, bg_ref, wu_ref, bu_ref, wd_ref, bd_ref, y_hbm,
                   wgb_ref, wub_ref, wdb_ref, xbuf_ref, obuf_ref, sem_in, sem_out):
    b = pl.program_id(0)
    nb = pl.num_programs(0)
    slot = b % 2

    def in_copy(blk, s):
        return pltpu.make_async_copy(xs_hbm.at[pl.ds(blk * MOE_BLOCK, MOE_BLOCK), 0], xbuf_ref.at[s], sem_in.at[s])

    def out_copy(blk, s):
        return pltpu.make_async_copy(obuf_ref.at[s], y_hbm.at[pl.ds(blk * MOE_BLOCK, MOE_BLOCK), 0], sem_out.at[s])

    @pl.when(b == 0)
    def _():
        in_copy(0, 0).start()

    nxt = jnp.minimum(b + 1, nb - 1)

    @pl.when(jnp.logical_and(b + 1 < nb, act_ref[nxt] == 1))
    def _():
        in_copy(b + 1, 1 - slot).start()

    @pl.when(b >= 2)
    def _():
        out_copy(b - 2, slot).wait()

    @pl.when(new_ref[b] == 1)
    def _():
        wgb_ref[...] = wg_ref[0].astype(BF16)
        wub_ref[...] = wu_ref[0].astype(BF16)
        wdb_ref[...] = wd_ref[0].astype(BF16)

    @pl.when(act_ref[b] == 1)
    def _():
        in_copy(b, slot).wait()
        p = xbuf_ref[slot]
        lo = pltpu.bitcast(p << 16, F32)
        hi = pltpu.bitcast(p & jnp.uint32(0xFFFF0000), F32)
        x = jnp.concatenate([lo, hi], axis=-1).astype(BF16)
        g = jnp.dot(x, wgb_ref[...], preferred_element_type=F32) + bg_ref[0]
        u = jnp.dot(x, wub_ref[...], preferred_element_type=F32) + bu_ref[0]
        g = jnp.minimum(g, SWIGLU_LIMIT)
        u = jnp.clip(u, -SWIGLU_LIMIT, SWIGLU_LIMIT)
        a = (u + 1.0) * (g * jax.nn.sigmoid(SWIGLU_ALPHA * g))
        obuf_ref[slot] = jnp.dot(a.astype(BF16), wdb_ref[...], preferred_element_type=F32) + bd_ref[0]

    @pl.when(act_ref[b] == 0)
    def _():
        obuf_ref[slot] = jnp.zeros(obuf_ref.shape[1:], F32)

    out_copy(b, slot).start()

    @pl.when(b == nb - 1)
    def _():
        out_copy(b - 1, 1 - slot).wait()
        out_copy(b, slot).wait()


def _experts(block_expert, block_active, block_new, xs, wg, bg, wu, bu, wd, bd):
    cap = xs.shape[0]
    n_blocks = cap // MOE_BLOCK
    assert n_blocks >= 2
    wspec = pl.BlockSpec((1, D_MODEL, D_MODEL), lambda b, be, act, new: (be[b], 0, 0))
    bspec = pl.BlockSpec((1, 1, D_MODEL), lambda b, be, act, new: (be[b], 0, 0))
    hbm = pl.BlockSpec(memory_space=pl.ANY)
    return pl.pallas_call(
        _expert_kernel,
        grid_spec=pltpu.PrefetchScalarGridSpec(
            num_scalar_prefetch=3, grid=(n_blocks,),
            in_specs=[hbm, wspec, bspec, wspec, bspec, wspec, bspec],
            out_specs=hbm,
            scratch_shapes=[pltpu.VMEM((D_MODEL, D_MODEL), BF16)] * 3
                           + [pltpu.VMEM((2, MOE_BLOCK, D_MODEL // 2), U32), pltpu.VMEM((2, MOE_BLOCK, D_MODEL), F32),
                              pltpu.SemaphoreType.DMA((2,)), pltpu.SemaphoreType.DMA((2,))]),
        out_shape=jax.ShapeDtypeStruct((cap, 1, D_MODEL), F32),
        compiler_params=_cparams("arbitrary"),
        name="moe_experts",
    )(block_expert, block_active, block_new, xs, wg, bg, wu, bu, wd, bd)


def _combine_kernel(hp_ref, hs_ref, gate_ref, dest_hbm, yb_hbm, g_ref, b_ref, yp_ref, ys_ref,
                    dest_smem, rows_ref, sem_idx, sem_rows, *, n_prompt):
    i = pl.program_id(0)
    n = TM_MOE * TOP_K
    slot = i % 2

    def gather_tile(tile, s):
        cp = pltpu.make_async_copy(dest_hbm.at[pl.ds(tile * n, n)], dest_smem.at[pl.ds(s * n, n)], sem_idx)
        cp.start()
        cp.wait()

        def issue(t, c):
            for k in range(TOP_K):
                pltpu.make_async_copy(yb_hbm.at[pl.ds(dest_smem[s * n + t * TOP_K + k], 1)],
                                      rows_ref.at[s, k, pl.ds(t, 1)], sem_rows.at[s]).start()
            return c

        lax.fori_loop(0, TM_MOE, issue, 0, unroll=2)

    @pl.when(i == 0)
    def _():
        gather_tile(0, 0)

    @pl.when(i + 1 < pl.num_programs(0))
    def _():
        gather_tile(i + 1, 1 - slot)

    pltpu.make_async_copy(rows_ref.at[slot], rows_ref.at[slot], sem_rows.at[slot]).wait()

    gate = gate_ref[...]
    ff = gate[:, 0:1] * rows_ref[slot, 0, :, 0, :]
    for k in range(1, TOP_K):
        ff = ff + gate[:, k:k + 1] * rows_ref[slot, k, :, 0, :]

    @pl.when(i < n_prompt)
    def _():
        yp_ref[...] = _layer_norm(DEEPNORM_ALPHA * hp_ref[...] + ff, g_ref[...], b_ref[...])

    @pl.when(i >= n_prompt)
    def _():
        ys_ref[...] = _layer_norm(DEEPNORM_ALPHA * hs_ref[...] + ff, g_ref[...], b_ref[...])


def _combine(h2_p, h2_s, gates, dest_flat, yb, ln_g, ln_b):
    n_prompt = h2_p.shape[0] // TM_MOE
    n_sample = h2_s.shape[0] // TM_MOE
    pmap = lambda i: (jnp.minimum(i, n_prompt - 1), 0)
    smap = lambda i: (jnp.maximum(i - n_prompt, 0), 0)
    vec = pl.BlockSpec((1, D_MODEL), lambda i: (0, 0))
    return pl.pallas_call(
        functools.partial(_combine_kernel, n_prompt=n_prompt),
        grid=(n_prompt + n_sample,),
        in_specs=[pl.BlockSpec((TM_MOE, D_MODEL), pmap), pl.BlockSpec((TM_MOE, D_MODEL), smap),
                  pl.BlockSpec((TM_MOE, TOP_K), lambda i: (i, 0)),
                  pl.BlockSpec(memory_space=pl.ANY), pl.BlockSpec(memory_space=pl.ANY), vec, vec],
        out_specs=[pl.BlockSpec((TM_MOE, D_MODEL), pmap), pl.BlockSpec((TM_MOE, D_MODEL), smap)],
        out_shape=[jax.ShapeDtypeStruct(h2_p.shape, F32), jax.ShapeDtypeStruct(h2_s.shape, F32)],
        scratch_shapes=[pltpu.SMEM((2 * TM_MOE * TOP_K,), I32),
                        pltpu.VMEM((2, TOP_K, TM_MOE, 1, D_MODEL), F32),
                        pltpu.SemaphoreType.DMA, pltpu.SemaphoreType.DMA((2,))],
        compiler_params=_cparams("arbitrary"),
        name="moe_combine",
    )(h2_p, h2_s, gates, dest_flat, yb, ln_g, ln_b)


def _tokens_major(a):
    n, k, tm = a.shape
    return jnp.transpose(a, (1, 0, 2)).reshape(k, n * tm)


def kernel(x_prompt, x_sample, mem_prompt, cache_diff_k, cache_diff_v, state_pool, cache_mem_k, cache_mem_v, w_in, lambda_q1, lambda_k1, lambda_q2, lambda_k2, subln_g, pool_w, pool_scale, w_out, ln1_g, ln1_b, xa_wq, xa_wk, xa_wv, xa_wo, ln2_g, ln2_b, router_w, router_b, moe_w_gate, moe_b_gate, moe_w_up, moe_b_up, moe_w_down, moe_b_down, ln3_g, ln3_b):
    assert w_in.shape[0] == DEPTH == 1
    batch, seq, d = x_prompt.shape
    dec_batch, dec_seq, _ = x_sample.shape
    past_len = cache_diff_k.shape[2]
    lambda_init = 0.8 - 0.6 * math.exp(-0.3 * 0)
    tp, ts = batch * seq, dec_batch * dec_seq

    vec = lambda a: a[0].reshape(1, -1)
    rw_t = router_w[0].T
    rw_hi = rw_t.astype(BF16)
    wts = dict(
        pool_w=pool_w[0].astype(BF16), pool_scale=vec(pool_scale), w_out=w_out[0].astype(BF16),
        ln1_g=vec(ln1_g), ln1_b=vec(ln1_b), xa_wq=xa_wq[0].astype(BF16), xa_wo=xa_wo[0].astype(BF16),
        ln2_g=vec(ln2_g), ln2_b=vec(ln2_b), rw_hi=rw_hi, rw_lo=(rw_t - rw_hi.astype(F32)).astype(BF16),
        router_b=router_b[0].reshape(N_EXPERTS, 1))
    w_in_bf = w_in[0].astype(BF16)
    lam_vecs = [vec(lambda_q1), vec(lambda_k1), vec(lambda_q2), vec(lambda_k2)]
    g_sub = vec(subln_g)

    xp2 = x_prompt.reshape(tp, d)
    qp, kp, vp, up, kpb, vpb = _inproj(xp2, w_in_bf, TM_PROJ)
    dp = _diff_prompt(qp, kpb, vpb, lam_vecs, g_sub, batch, seq, lambda_init)
    mk, mv, mkb, mvb = _memkv(mem_prompt.reshape(batch * N_MEM, d), xa_wk[0].astype(BF16),
                              xa_wv[0].astype(BF16), TM_PROJ)
    per16 = TM_POST // HIST_ROWS
    h2_p, h2p_p, idx_p, gate_p = _post(
        dp, up, up, lambda i: (jnp.maximum(i * per16 - 1, 0), 0), xp2, mkb, mvb, wts,
        tm=TM_POST, seq=seq, pos0=0, zero_first_hist=True)

    xs2 = x_sample.reshape(ts, d)
    qs, ks, vs, us, ksb, vsb = _inproj(xs2, w_in_bf, TM_PROJ)
    ds = _diff_sample(qs, cache_diff_k, cache_diff_v, ksb, vsb, lam_vecs, g_sub, dec_batch, dec_seq, lambda_init)
    hist_s = jnp.pad(state_pool[0], ((0, 0), (1, 0), (0, 0))).reshape(dec_batch * HIST_ROWS, POOL_WIDTH)
    h2_s, h2p_s, idx_s, gate_s = _post(
        ds, us, hist_s, lambda i: (i, 0), xs2,
        cache_mem_k[0].reshape(dec_batch * N_MEM, d).astype(BF16),
        cache_mem_v[0].reshape(dec_batch * N_MEM, d).astype(BF16), wts,
        tm=dec_seq, seq=dec_seq, pos0=past_len, zero_first_hist=False)

    t_all = tp + ts
    idx_all = jnp.concatenate([_tokens_major(idx_p), _tokens_major(idx_s)], axis=1)
    gate_all = jnp.concatenate([_tokens_major(gate_p), _tokens_major(gate_s)], axis=1)
    rank_all, counts = _ranks(idx_all)
    counts = counts[:, 0]
    padded = (counts + MOE_BLOCK - 1) // MOE_BLOCK * MOE_BLOCK
    pad_end = jnp.cumsum(padded)
    pad_start = (pad_end - padded).astype(I32)
    n_blocks = t_all * TOP_K // MOE_BLOCK + N_EXPERTS
    block_start = jnp.arange(n_blocks, dtype=I32) * MOE_BLOCK
    block_expert = jnp.minimum(jnp.sum(block_start[:, None] >= pad_end[None, :], axis=1), N_EXPERTS - 1).astype(I32)
    block_active = (block_start < pad_end[-1]).astype(I32)
    block_new = jnp.concatenate([jnp.ones((1,), I32), (block_expert[1:] != block_expert[:-1]).astype(I32)])
    dest_flat = _dests(pad_start, idx_all, rank_all).T.reshape(-1)
    xs = _dispatch(counts, pad_start, h2p_p, h2p_s, dest_flat, n_blocks * MOE_BLOCK)
    yb = _experts(block_expert, block_active, block_new, xs,
                  moe_w_gate[0], moe_b_gate[0].reshape(N_EXPERTS, 1, D_MODEL),
                  moe_w_up[0], moe_b_up[0].reshape(N_EXPERTS, 1, D_MODEL),
                  moe_w_down[0], moe_b_down[0].reshape(N_EXPERTS, 1, D_MODEL))
    y_p, y_s = _combine(h2_p, h2_s, gate_all.T, dest_flat, yb, vec(ln3_g), vec(ln3_b))

    heads = (N_DIFF_HEADS, DIFF_V_DIM)
    xa = (N_XA_HEADS, XA_HEAD_DIM)
    up3 = up.reshape(batch, seq, POOL_WIDTH)
    us3 = us.reshape(dec_batch, dec_seq, POOL_WIDTH)
    pool_s = jnp.concatenate([state_pool[0].astype(F32), us3], axis=1)[:, -POOL_HIST:]
    return (y_p.reshape(batch, seq, d), y_s.reshape(dec_batch, dec_seq, d),
            kp.reshape(1, batch, seq, *heads), vp.reshape(1, batch, seq, *heads),
            up3[:, seq - POOL_HIST:][None],
            mk.reshape(1, batch, N_MEM, *xa), mv.reshape(1, batch, N_MEM, *xa),
            ks.reshape(1, dec_batch, dec_seq, *heads), vs.reshape(1, dec_batch, dec_seq, *heads),
            pool_s[None])
```

```python
import functools
import math

import jax
import jax.numpy as jnp
from jax import lax
from jax.experimental import pallas as pl
from jax.experimental.pallas import tpu as pltpu

F32 = jnp.float32
BF16 = jnp.bfloat16
I32 = jnp.int32
U32 = jnp.uint32

D_MODEL = 1024
CHUNK = 64
N_DIFF_HEADS = 4
DIFF_HEAD_DIM = 64
DIFF_V_DIM = 2 * DIFF_HEAD_DIM
DIFF_WIDTH = N_DIFF_HEADS * DIFF_V_DIM
POOL_WIDTH = D_MODEL - DIFF_WIDTH
POOL_WINDOWS = (2, 4, 8, 16)
POOL_GROUP_DIM = POOL_WIDTH // len(POOL_WINDOWS)
POOL_HIST = max(POOL_WINDOWS) - 1
HIST_ROWS = POOL_HIST + 1
IN_WIDTH = 3 * DIFF_WIDTH + POOL_WIDTH
N_MEM = 256
N_XA_HEADS = 4
XA_HEAD_DIM = D_MODEL // N_XA_HEADS
N_EXPERTS = 32
TOP_K = 4
SWIGLU_LIMIT = 7.0
SWIGLU_ALPHA = 1.702
MOE_BLOCK = 256
LN_EPS = 1e-5
RMS_EPS = 1e-5
DEPTH = 1
DEEPNORM_ALPHA = (2.0 * DEPTH) ** 0.25
NEG_BIG = -1e30
LOG2_E = math.log2(math.e)

VMEM_LIMIT = 56 * 1024 * 1024

TM_PROJ = 512
TQ_ATT = 256
TK_ATT = 512
TK_PAST = 1024
TM_POST = 512
TR_RANK = 512
TM_MOE = 256


def _cparams(*sem):
    return pltpu.CompilerParams(dimension_semantics=sem, vmem_limit_bytes=VMEM_LIMIT)


def _layer_norm(x, g, b):
    mu = jnp.mean(x, axis=-1, keepdims=True)
    xc = x - mu
    var = jnp.mean(xc * xc, axis=-1, keepdims=True)
    return xc * lax.rsqrt(var + LN_EPS) * g + b


def _inproj_kernel(x_ref, w_ref, q_ref, k_hbm, v_hbm, u_ref, kb_ref, vb_ref, kv_ref, sem):
    i = pl.program_id(0)
    n = pl.num_programs(0)
    tm = x_ref.shape[0]
    slot = i % 2

    def cache_copy(step, s, which, dst_hbm):
        return pltpu.make_async_copy(kv_ref.at[s, which], dst_hbm.at[pl.ds(step * tm, tm), 0], sem.at[s, which])

    @pl.when(i >= 2)
    def _():
        cache_copy(i - 2, slot, 0, k_hbm).wait()
        cache_copy(i - 2, slot, 1, v_hbm).wait()

    x = x_ref[...].astype(BF16)
    w = DIFF_WIDTH
    q = jnp.dot(x, w_ref[:, 0:w], preferred_element_type=F32)
    q_ref[...] = (q * (DIFF_HEAD_DIM ** -0.5 * LOG2_E)).astype(BF16)
    k = jnp.dot(x, w_ref[:, w:2 * w], preferred_element_type=F32)
    kb_ref[...] = k.astype(BF16)
    v = jnp.dot(x, w_ref[:, 2 * w:3 * w], preferred_element_type=F32)
    vb_ref[...] = v.astype(BF16)
    kv_ref[slot, 0] = k
    kv_ref[slot, 1] = v
    cache_copy(i, slot, 0, k_hbm).start()
    cache_copy(i, slot, 1, v_hbm).start()
    u_ref[...] = jnp.dot(x, w_ref[:, 3 * w:], preferred_element_type=F32)

    @pl.when(i == n - 1)
    def _():
        @pl.when(n >= 2)
        def _():
            cache_copy(i - 1, 1 - slot, 0, k_hbm).wait()
            cache_copy(i - 1, 1 - slot, 1, v_hbm).wait()

        cache_copy(i, slot, 0, k_hbm).wait()
        cache_copy(i, slot, 1, v_hbm).wait()


def _inproj(x2d, w_in_bf, tm):
    t = x2d.shape[0]
    w = DIFF_WIDTH
    row = lambda i: (i, 0)
    out_spec = pl.BlockSpec((tm, w), row)
    head_spec = pl.BlockSpec(memory_space=pl.ANY)
    head_shape = jax.ShapeDtypeStruct((t, 1, w), F32)
    return pl.pallas_call(
        _inproj_kernel,
        grid=(t // tm,),
        in_specs=[pl.BlockSpec((tm, D_MODEL), row),
                  pl.BlockSpec((D_MODEL, IN_WIDTH), lambda i: (0, 0))],
        out_specs=[out_spec, head_spec, head_spec, out_spec, out_spec, out_spec],
        out_shape=[jax.ShapeDtypeStruct((t, w), BF16), head_shape, head_shape,
                   jax.ShapeDtypeStruct((t, w), F32),
                   jax.ShapeDtypeStruct((t, w), BF16), jax.ShapeDtypeStruct((t, w), BF16)],
        scratch_shapes=[pltpu.VMEM((2, 2, tm, w), F32), pltpu.SemaphoreType.DMA((2, 2))],
        compiler_params=_cparams("arbitrary"),
        name="inproj",
    )(x2d, w_in_bf)


def _memkv_kernel(x_ref, wk_ref, wv_ref, k_ref, v_ref, kb_ref, vb_ref):
    x = x_ref[...].astype(BF16)
    k = jnp.dot(x, wk_ref[...], preferred_element_type=F32)
    kb_ref[...] = k.astype(BF16)
    v = jnp.dot(x, wv_ref[...], preferred_element_type=F32)
    vb_ref[...] = v.astype(BF16)
    for h in range(N_XA_HEADS):
        cs = slice(h * XA_HEAD_DIM, (h + 1) * XA_HEAD_DIM)
        k_ref[:, h, :] = k[:, cs]
        v_ref[:, h, :] = v[:, cs]


def _memkv(mem2d, wk_bf, wv_bf, tm):
    t = mem2d.shape[0]
    row = lambda i: (i, 0)
    full = lambda i: (0, 0)
    spec = pl.BlockSpec((tm, D_MODEL), row)
    head_spec = pl.BlockSpec((tm, N_XA_HEADS, XA_HEAD_DIM), lambda i: (i, 0, 0))
    head_shape = jax.ShapeDtypeStruct((t, N_XA_HEADS, XA_HEAD_DIM), F32)
    return pl.pallas_call(
        _memkv_kernel,
        grid=(t // tm,),
        in_specs=[spec, pl.BlockSpec((D_MODEL, D_MODEL), full), pl.BlockSpec((D_MODEL, D_MODEL), full)],
        out_specs=[head_spec, head_spec, spec, spec],
        out_shape=[head_shape, head_shape,
                   jax.ShapeDtypeStruct((t, D_MODEL), BF16), jax.ShapeDtypeStruct((t, D_MODEL), BF16)],
        compiler_params=_cparams("parallel"),
        name="memkv",
    )(mem2d, wk_bf, wv_bf)


def _diff_lambda(lq1_ref, lk1_ref, lq2_ref, lk2_ref, lambda_init):
    s1 = jnp.sum(lq1_ref[...] * lk1_ref[...], axis=-1, keepdims=True)
    s2 = jnp.sum(lq2_ref[...] * lk2_ref[...], axis=-1, keepdims=True)
    return jnp.exp(s1) - jnp.exp(s2) + lambda_init


def _stack_maps(qh):
    lane = lax.broadcasted_iota(I32, qh.shape, 1)
    zero = jnp.zeros_like(qh)
    return jnp.concatenate([jnp.where(lane < DIFF_HEAD_DIM, qh, zero),
                            jnp.where(lane >= DIFF_HEAD_DIM, qh, zero)], axis=0)


def _lanes(x, n):
    w = x.shape[1]
    return x[:, :n] if n <= w else jnp.tile(x, (1, n // w))


def _softmax_step(qq, kh, vh, m_ref, acc_ref, mask):
    s = lax.dot_general(qq, kh, (((1,), (1,)), ((), ())), preferred_element_type=F32)
    if mask is not None:
        s = jnp.where(mask, s, NEG_BIG)
    m_old = m_ref[...]
    m_new = jnp.maximum(m_old, jnp.max(s, axis=-1, keepdims=True))
    p = jnp.exp2(s - _lanes(m_new, s.shape[1])).astype(BF16)
    ones_col = (lax.broadcasted_iota(I32, vh.shape, 1) == 0).astype(BF16)
    pv = jnp.dot(p, jnp.concatenate([vh, ones_col], axis=1), preferred_element_type=F32)
    a = jnp.exp2(m_old - m_new)
    acc_ref[...] = _lanes(a, pv.shape[1]) * acc_ref[...] + pv
    m_ref[...] = m_new


def _diff_finish(acc, lam, g, lambda_init, tq):
    hd = DIFF_V_DIM
    o = acc[:tq, :hd] / acc[:tq, hd:hd + 1] - lam * (acc[tq:, :hd] / acc[tq:, hd:hd + 1])
    o = o * lax.rsqrt(jnp.mean(o * o, axis=-1, keepdims=True) + RMS_EPS)
    return o * g * (1.0 - lambda_init)


def _diff_prompt_kernel(q_ref, k_ref, v_ref, lq1_ref, lk1_ref, lq2_ref, lk2_ref, g_ref, o_ref, m_ref, acc_ref,
                        *, lambda_init):
    i = pl.program_id(1)
    tq, tk, hd = TQ_ATT, TK_ATT, DIFF_V_DIM
    n_full = (i * tq) // tk
    qrow = lax.broadcasted_iota(I32, (2 * tq, tk), 0)
    qrow = jnp.where(qrow >= tq, qrow - tq, qrow) + i * tq
    kcol = lax.broadcasted_iota(I32, (2 * tq, tk), 1) + n_full * tk
    last_mask = (kcol // CHUNK) <= (qrow // CHUNK)
    m_ref[...] = jnp.full(m_ref.shape, NEG_BIG, F32)
    acc_ref[...] = jnp.zeros(acc_ref.shape, F32)

    def step(j, mask):
        r0 = pl.multiple_of(j * tk, tk)
        for h in range(N_DIFF_HEADS):
            cs = slice(h * hd, (h + 1) * hd)
            _softmax_step(_stack_maps(q_ref[:, cs]), k_ref[pl.ds(r0, tk), cs], v_ref[pl.ds(r0, tk), cs],
                          m_ref.at[h], acc_ref.at[h], mask)

    def body(j, c):
        step(j, None)
        return c

    lax.fori_loop(0, n_full, body, 0)
    step(n_full, last_mask)
    lam = _diff_lambda(lq1_ref, lk1_ref, lq2_ref, lk2_ref, lambda_init)
    for h in range(N_DIFF_HEADS):
        o_ref[:, h * hd:(h + 1) * hd] = _diff_finish(acc_ref[h], lam, g_ref[...], lambda_init,
                                                     tq).astype(o_ref.dtype)


def _diff_prompt(qb, kb, vb, lam_vecs, subln_g, batch, seq, lambda_init):
    assert TQ_ATT % CHUNK == 0 and TK_ATT % TQ_ATT == 0 and seq % TK_ATT == 0
    nq = seq // TQ_ATT
    w = DIFF_WIDTH
    small = lambda b, i: (0, 0)
    return pl.pallas_call(
        functools.partial(_diff_prompt_kernel, lambda_init=lambda_init),
        grid=(batch, nq),
        in_specs=[pl.BlockSpec((TQ_ATT, w), lambda b, i: (b * nq + i, 0)),
                  pl.BlockSpec((seq, w), lambda b, i: (b, 0)),
                  pl.BlockSpec((seq, w), lambda b, i: (b, 0))]
                 + [pl.BlockSpec((1, DIFF_HEAD_DIM), small)] * 4
                 + [pl.BlockSpec((1, DIFF_V_DIM), small)],
        out_specs=pl.BlockSpec((TQ_ATT, w), lambda b, i: (b * nq + i, 0)),
        out_shape=jax.ShapeDtypeStruct((batch * seq, w), BF16),
        scratch_shapes=[pltpu.VMEM((N_DIFF_HEADS, 2 * TQ_ATT, 128), F32),
                        pltpu.VMEM((N_DIFF_HEADS, 2 * TQ_ATT, 2 * DIFF_V_DIM), F32)],
        compiler_params=_cparams("parallel", "arbitrary"),
        name="diff_attn_prompt",
    )(qb, kb, vb, *lam_vecs, subln_g)


def _diff_sample_kernel(q_ref, pk_hbm, pv_hbm, nk_ref, nv_ref, lq1_ref, lk1_ref, lq2_ref, lk2_ref, g_ref,
                        o_ref, m_ref, acc_ref, kv_ref, sem, *, lambda_init, n_past):
    b = pl.program_id(0)
    j = pl.program_id(1)
    tq, hd = q_ref.shape[0], DIFF_V_DIM
    tk = kv_ref.shape[2]
    n_tiles = pl.num_programs(0) * n_past

    def past_copy(tile, which, src_hbm):
        s = tile % 2
        return pltpu.make_async_copy(src_hbm.at[pl.ds(tile * tk, tk), 0], kv_ref.at[s, which], sem.at[s, which])

    def fetch(tile):
        past_copy(tile, 0, pk_hbm).start()
        past_copy(tile, 1, pv_hbm).start()

    @pl.when(jnp.logical_and(b == 0, j == 0))
    def _():
        fetch(0)

    @pl.when(j == 0)
    def _():
        m_ref[...] = jnp.full(m_ref.shape, NEG_BIG, F32)
        acc_ref[...] = jnp.zeros(acc_ref.shape, F32)

    def update(k_of, v_of):
        for h in range(N_DIFF_HEADS):
            qq = _stack_maps(q_ref[:, h * hd:(h + 1) * hd])
            _softmax_step(qq, k_of(h), v_of(h), m_ref.at[h], acc_ref.at[h], None)

    @pl.when(j < n_past)
    def _():
        cur = b * n_past + j

        @pl.when(cur + 1 < n_tiles)
        def _():
            fetch(cur + 1)

        past_copy(cur, 0, pk_hbm).wait()
        past_copy(cur, 1, pv_hbm).wait()
        s = cur % 2
        update(lambda h: kv_ref[s, 0, :, h * hd:(h + 1) * hd].astype(BF16),
               lambda h: kv_ref[s, 1, :, h * hd:(h + 1) * hd].astype(BF16))

    @pl.when(j == n_past)
    def _():
        update(lambda h: nk_ref[:, h * hd:(h + 1) * hd], lambda h: nv_ref[:, h * hd:(h + 1) * hd])
        lam = _diff_lambda(lq1_ref, lk1_ref, lq2_ref, lk2_ref, lambda_init)
        for h in range(N_DIFF_HEADS):
            o_ref[:, h * hd:(h + 1) * hd] = _diff_finish(acc_ref[h], lam, g_ref[...], lambda_init,
                                                         tq).astype(o_ref.dtype)


def _diff_sample(qb, past_k, past_v, kb, vb, lam_vecs, subln_g, batch, seq, lambda_init):
    past_len = past_k.shape[0] // batch
    assert past_len % TK_PAST == 0
    n_past = past_len // TK_PAST
    w = DIFF_WIDTH
    small = lambda b, j: (0, 0)
    hbm = pl.BlockSpec(memory_space=pl.ANY)
    new_spec = pl.BlockSpec((seq, w), lambda b, j: (b, 0))
    return pl.pallas_call(
        functools.partial(_diff_sample_kernel, lambda_init=lambda_init, n_past=n_past),
        grid=(batch, n_past + 1),
        in_specs=[new_spec, hbm, hbm, new_spec, new_spec]
                 + [pl.BlockSpec((1, DIFF_HEAD_DIM), small)] * 4
                 + [pl.BlockSpec((1, DIFF_V_DIM), small)],
        out_specs=new_spec,
        out_shape=jax.ShapeDtypeStruct((batch * seq, w), BF16),
        scratch_shapes=[pltpu.VMEM((N_DIFF_HEADS, 2 * seq, 128), F32),
                        pltpu.VMEM((N_DIFF_HEADS, 2 * seq, 2 * DIFF_V_DIM), F32),
                        pltpu.VMEM((2, 2, TK_PAST, w), F32), pltpu.SemaphoreType.DMA((2, 2))],
        compiler_params=_cparams("arbitrary", "arbitrary"),
        name="diff_attn_sample",
    )(qb, past_k, past_v, kb, vb, *lam_vecs, subln_g)


def _post_kernel(d_ref, u_ref, hist_ref, x_ref, pw_ref, ps_ref, wo1_ref, g1_ref, b1_ref, wq_ref, mk_ref,
                 mv_ref, wo2_ref, g2_ref, b2_ref, rw_ref, rb_ref,
                 h2_ref, h2p_ref, idx_ref, gate_ref, full_ref, tmp_ref, *, tm, seq, pos0, zero_first_hist):
    i = pl.program_id(0)
    row0 = (i * tm) % seq

    hist = hist_ref[...]
    if zero_first_hist:
        hist = jnp.where(row0 == 0, jnp.zeros_like(hist), hist)
    full_ref[0:HIST_ROWS, :] = hist
    u = u_ref[...]
    full_ref[HIST_ROWS:, :] = u
    pos = pos0 + row0 + lax.broadcasted_iota(I32, (tm, 1), 0)
    n_end = HIST_ROWS + tm

    def trailing_sum(cs, w):
        read = lambda a, b: full_ref[a:b, cs]
        lo, sh, level = HIST_ROWS - (w - 2), 1, 0
        while True:
            out = read(lo, n_end) + read(lo - sh, n_end - sh)
            sh *= 2
            if sh == w:
                return out
            tmp_ref[level % 2, 0:n_end - lo, :] = out
            read = lambda a, b, base=lo, buf=level % 2: tmp_ref[buf, a - base:b - base, :]
            lo, level = lo + sh, level + 1

    pooled = []
    for gi, w in enumerate(POOL_WINDOWS):
        cs = slice(gi * POOL_GROUP_DIM, (gi + 1) * POOL_GROUP_DIM)
        inv_cnt = 1.0 / jnp.minimum(w, pos + 1).astype(F32)
        m = trailing_sum(cs, w) * inv_cnt - u[:, cs]
        y = jnp.dot(m.astype(BF16), pw_ref[gi], preferred_element_type=F32)
        pooled.append((y * ps_ref[:, cs]).astype(BF16))
    mixed_in = jnp.concatenate([d_ref[...]] + pooled, axis=-1)

    mix = jnp.dot(mixed_in, wo1_ref[...], preferred_element_type=F32)
    h1 = _layer_norm(DEEPNORM_ALPHA * x_ref[...] + mix, g1_ref[...], b1_ref[...])

    q = jnp.dot(h1.astype(BF16), wq_ref[...], preferred_element_type=F32)
    qb = (q * (XA_HEAD_DIM ** -0.5)).astype(BF16)
    heads = []
    for h in range(N_XA_HEADS):
        cs = slice(h * XA_HEAD_DIM, (h + 1) * XA_HEAD_DIM)
        s = lax.dot_general(qb[:, cs], mk_ref[:, cs], (((1,), (1,)), ((), ())), preferred_element_type=F32)
        e = jnp.exp(s - jnp.max(s, axis=-1, keepdims=True))
        p = e / jnp.sum(e, axis=-1, keepdims=True)
        heads.append(jnp.dot(p.astype(BF16), mv_ref[:, cs], preferred_element_type=F32).astype(BF16))
    ca = jnp.dot(jnp.concatenate(heads, axis=-1), wo2_ref[...], preferred_element_type=F32)
    h2 = _layer_norm(DEEPNORM_ALPHA * h1 + ca, g2_ref[...], b2_ref[...])
    h2_ref[...] = h2

    hb = h2.astype(BF16)
    bits = pltpu.bitcast(hb.astype(F32), U32)
    half = D_MODEL // 2
    h2p_ref[...] = (bits[:, :half] >> 16) | (bits[:, half:] & jnp.uint32(0xFFFF0000))

    logits = lax.dot_general(rw_ref[...], hb, (((1,), (1,)), ((), ())), preferred_element_type=F32) + rb_ref[...]
    erow = lax.broadcasted_iota(I32, logits.shape, 0)
    vals, idxs = [], []
    for _ in range(TOP_K):
        mx = jnp.max(logits, axis=0, keepdims=True)
        ix = jnp.min(jnp.where(logits == mx, erow, N_EXPERTS), axis=0, keepdims=True)
        vals.append(mx)
        idxs.append(ix)
        logits = jnp.where(erow == ix, -jnp.inf, logits)
    ex = [jnp.exp(v - vals[0]) for v in vals]
    den = ex[0] + ex[1] + ex[2] + ex[3]
    idx_ref[0] = jnp.concatenate(idxs, axis=0)
    gate_ref[0] = jnp.concatenate([e / den for e in ex], axis=0)


def _post(diff_out, u, hist_arr, hist_map, x2d, mkb, mvb, wts, *, tm, seq, pos0, zero_first_hist):
    t = x2d.shape[0]
    n = t // tm
    assert seq % tm == 0 or tm % seq == 0
    row = lambda i: (i, 0)
    full2 = lambda i: (0, 0)
    mem_map = lambda i: ((i * tm) // seq, 0)
    vec = pl.BlockSpec((1, D_MODEL), full2)
    wspec = pl.BlockSpec((D_MODEL, D_MODEL), full2)
    in_specs = [
        pl.BlockSpec((tm, DIFF_WIDTH), row), pl.BlockSpec((tm, POOL_WIDTH), row),
        pl.BlockSpec((HIST_ROWS, POOL_WIDTH), hist_map), pl.BlockSpec((tm, D_MODEL), row),
        pl.BlockSpec((len(POOL_WINDOWS), POOL_GROUP_DIM, POOL_GROUP_DIM), lambda i: (0, 0, 0)),
        pl.BlockSpec((1, POOL_WIDTH), full2),
        wspec, vec, vec,
        wspec, pl.BlockSpec((N_MEM, D_MODEL), mem_map), pl.BlockSpec((N_MEM, D_MODEL), mem_map),
        wspec, vec, vec,
        pl.BlockSpec((N_EXPERTS, D_MODEL), full2), pl.BlockSpec((N_EXPERTS, 1), full2),
    ]
    out_specs = [pl.BlockSpec((tm, D_MODEL), row), pl.BlockSpec((tm, D_MODEL // 2), row),
                 pl.BlockSpec((1, TOP_K, tm), lambda i: (i, 0, 0)),
                 pl.BlockSpec((1, TOP_K, tm), lambda i: (i, 0, 0))]
    out_shape = [jax.ShapeDtypeStruct((t, D_MODEL), F32), jax.ShapeDtypeStruct((t, D_MODEL // 2), U32),
                 jax.ShapeDtypeStruct((n, TOP_K, tm), I32), jax.ShapeDtypeStruct((n, TOP_K, tm), F32)]
    return pl.pallas_call(
        functools.partial(_post_kernel, tm=tm, seq=seq, pos0=pos0, zero_first_hist=zero_first_hist),
        grid=(n,),
        in_specs=in_specs,
        out_specs=out_specs,
        out_shape=out_shape,
        scratch_shapes=[pltpu.VMEM((HIST_ROWS + tm, POOL_WIDTH), F32),
                        pltpu.VMEM((2, HIST_ROWS + tm, POOL_GROUP_DIM), F32)],
        compiler_params=_cparams("parallel"),
        name="post_attn",
    )(diff_out, u, hist_arr, x2d, wts["pool_w"], wts["pool_scale"], wts["w_out"], wts["ln1_g"], wts["ln1_b"],
      wts["xa_wq"], mkb, mvb, wts["xa_wo"], wts["ln2_g"], wts["ln2_b"], wts["router_wt"], wts["router_b"])


def _rank_kernel(idx_ref, rank_ref, cnt_ref, carry_ref):
    @pl.when(pl.program_id(0) == 0)
    def _():
        carry_ref[...] = jnp.zeros(carry_ref.shape, F32)

    idx = idx_ref[...]
    tr = idx.shape[1]
    erow = lax.broadcasted_iota(I32, (N_EXPERTS, tr), 0)
    hits = [erow == idx[k:k + 1, :] for k in range(TOP_K)]
    onehot = sum(h.astype(F32) for h in hits)
    earlier = (lax.broadcasted_iota(I32, (tr, tr), 0) < lax.broadcasted_iota(I32, (tr, tr), 1)).astype(BF16)
    before = jnp.dot(onehot.astype(BF16), earlier, preferred_element_type=F32) + carry_ref[:, 0:1]
    ranks = [jnp.sum(jnp.where(h, before, 0.0), axis=0, keepdims=True) for h in hits]
    rank_ref[...] = jnp.concatenate(ranks, axis=0).astype(I32)
    carry_ref[...] = carry_ref[...] + jnp.sum(onehot, axis=1, keepdims=True)
    cnt_ref[...] = carry_ref[...].astype(I32)


def _ranks(idx_all):
    t = idx_all.shape[1]
    assert t % TR_RANK == 0
    return pl.pallas_call(
        _rank_kernel,
        grid=(t // TR_RANK,),
        in_specs=[pl.BlockSpec((TOP_K, TR_RANK), lambda i: (0, i))],
        out_specs=[pl.BlockSpec((TOP_K, TR_RANK), lambda i: (0, i)),
                   pl.BlockSpec((N_EXPERTS, 128), lambda i: (0, 0))],
        out_shape=[jax.ShapeDtypeStruct((TOP_K, t), I32), jax.ShapeDtypeStruct((N_EXPERTS, 128), I32)],
        scratch_shapes=[pltpu.VMEM((N_EXPERTS, 128), F32)],
        compiler_params=_cparams("arbitrary"),
        name="moe_rank",
    )(idx_all)


def _dest_kernel(start_ref, idx_ref, rank_ref, dest_ref):
    idx = idx_ref[...]
    base = jnp.zeros(idx.shape, I32)
    for e in range(N_EXPERTS):
        base = jnp.where(idx == e, start_ref[e], base)
    dest_ref[...] = base + rank_ref[...]


def _dests(pad_start, idx_all, rank_all):
    t = idx_all.shape[1]
    spec = pl.BlockSpec((TOP_K, TR_RANK), lambda i, s: (0, i))
    return pl.pallas_call(
        _dest_kernel,
        grid_spec=pltpu.PrefetchScalarGridSpec(
            num_scalar_prefetch=1, grid=(t // TR_RANK,), in_specs=[spec, spec], out_specs=spec),
        out_shape=jax.ShapeDtypeStruct((TOP_K, t), I32),
        compiler_params=_cparams("parallel"),
        name="moe_dest",
    )(pad_start, idx_all, rank_all)


def _row_copy(src_ref, src_row, dst_ref, dst_row, sem):
    return pltpu.make_async_copy(src_ref.at[pl.ds(src_row, 1)], dst_ref.at[pl.ds(dst_row, 1), 0], sem)


def _dispatch_kernel(cnt_ref, start_ref, hp_ref, hs_ref, dest_hbm, xs_out, dest_smem, zero_ref,
                     sem_idx, sem_rows, sem_fill, *, n_prompt):
    i = pl.program_id(0)
    n = TM_MOE * TOP_K
    cp = pltpu.make_async_copy(dest_hbm.at[pl.ds(i * n, n)], dest_smem, sem_idx)
    cp.start()

    @pl.when(i == 0)
    def _():
        zero_ref[...] = jnp.zeros(zero_ref.shape, U32)
        for e in range(N_EXPERTS):
            n_pad = (cnt_ref[e] + MOE_BLOCK - 1) // MOE_BLOCK * MOE_BLOCK - cnt_ref[e]
            first = start_ref[e] + cnt_ref[e]

            def fill(r, c):
                _row_copy(zero_ref, 0, xs_out, first + r, sem_fill).start()
                return c

            lax.fori_loop(0, n_pad, fill, 0)

            def drain_fill(r, c):
                _row_copy(zero_ref, 0, xs_out, 0, sem_fill).wait()
                return c

            lax.fori_loop(0, n_pad, drain_fill, 0)

        last = N_EXPERTS - 1
        n_used = (start_ref[last] + cnt_ref[last] + MOE_BLOCK - 1) // MOE_BLOCK
        n_blocks = xs_out.shape[0] // MOE_BLOCK

        def block_copy(b):
            return pltpu.make_async_copy(zero_ref, xs_out.at[pl.ds(b * MOE_BLOCK, MOE_BLOCK), 0], sem_fill)

        def fill_block(b, c):
            block_copy(b).start()
            return c

        lax.fori_loop(n_used, n_blocks, fill_block, 0)

        def drain_block(b, c):
            block_copy(b).wait()
            return c

        lax.fori_loop(n_used, n_blocks, drain_block, 0)

    cp.wait()

    def scatter_from(src_ref):
        def issue(t, c):
            for k in range(TOP_K):
                _row_copy(src_ref, t, xs_out, dest_smem[t * TOP_K + k], sem_rows).start()
            return c

        lax.fori_loop(0, TM_MOE, issue, 0, unroll=2)

    @pl.when(i < n_prompt)
    def _():
        scatter_from(hp_ref)

    @pl.when(i >= n_prompt)
    def _():
        scatter_from(hs_ref)

    tile_rows = xs_out.at[pl.ds(0, n)]
    pltpu.make_async_copy(tile_rows, tile_rows, sem_rows).wait()


def _dispatch(counts, pad_start, h2p_p, h2p_s, dest_flat, cap):
    n_prompt = h2p_p.shape[0] // TM_MOE
    n_sample = h2p_s.shape[0] // TM_MOE
    half = D_MODEL // 2
    hbm = pl.BlockSpec(memory_space=pl.ANY)
    return pl.pallas_call(
        functools.partial(_dispatch_kernel, n_prompt=n_prompt),
        grid_spec=pltpu.PrefetchScalarGridSpec(
            num_scalar_prefetch=2, grid=(n_prompt + n_sample,),
            in_specs=[pl.BlockSpec((TM_MOE, half), lambda i, c, s: (jnp.minimum(i, n_prompt - 1), 0)),
                      pl.BlockSpec((TM_MOE, half), lambda i, c, s: (jnp.maximum(i - n_prompt, 0), 0)),
                      hbm],
            out_specs=hbm,
            scratch_shapes=[pltpu.SMEM((TM_MOE * TOP_K,), I32), pltpu.VMEM((MOE_BLOCK, half), U32),
                            pltpu.SemaphoreType.DMA, pltpu.SemaphoreType.DMA, pltpu.SemaphoreType.DMA]),
        out_shape=jax.ShapeDtypeStruct((cap, 1, half), U32),
        compiler_params=_cparams("arbitrary"),
        name="moe_dispatch",
    )(counts, pad_start, h2p_p, h2p_s, dest_flat)


def _expert_kernel(be_ref, act_ref, new_ref, xs_hbm, wg_ref, bg_ref, wu_ref, bu_ref, wd_ref, bd_ref, y_hbm,
                   wgb_ref, wub_ref, wdb_ref, xbuf_ref, obuf_ref, sem_in, sem_out):
    b = pl.program_id(0)
    nb = pl.num_programs(0)
    slot = b % 2

    def in_copy(blk, s):
        return pltpu.make_async_copy(xs_hbm.at[pl.ds(blk * MOE_BLOCK, MOE_BLOCK), 0], xbuf_ref.at[s], sem_in.at[s])

    def out_copy(blk, s):
        return pltpu.make_async_copy(obuf_ref.at[s], y_hbm.at[pl.ds(blk * MOE_BLOCK, MOE_BLOCK), 0], sem_out.at[s])

    @pl.when(b == 0)
    def _():
        in_copy(0, 0).start()

    nxt = jnp.minimum(b + 1, nb - 1)

    @pl.when(jnp.logical_and(b + 1 < nb, act_ref[nxt] == 1))
    def _():
        in_copy(b + 1, 1 - slot).start()

    @pl.when(b >= 2)
    def _():
        out_copy(b - 2, slot).wait()

    @pl.when(new_ref[b] == 1)
    def _():
        wgb_ref[...] = wg_ref[0].astype(BF16)
        wub_ref[...] = wu_ref[0].astype(BF16)
        wdb_ref[...] = wd_ref[0].astype(BF16)

    @pl.when(act_ref[b] == 1)
    def _():
        in_copy(b, slot).wait()
        p = xbuf_ref[slot]
        lo = pltpu.bitcast(p << 16, F32)
        hi = pltpu.bitcast(p & jnp.uint32(0xFFFF0000), F32)
        x = jnp.concatenate([lo, hi], axis=-1).astype(BF16)
        g = jnp.dot(x, wgb_ref[...], preferred_element_type=F32) + bg_ref[0]
        u = jnp.dot(x, wub_ref[...], preferred_element_type=F32) + bu_ref[0]
        g = jnp.minimum(g, SWIGLU_LIMIT)
        u = jnp.clip(u, -SWIGLU_LIMIT, SWIGLU_LIMIT)
        a = (u + 1.0) * (g * jax.nn.sigmoid(SWIGLU_ALPHA * g))
        obuf_ref[slot] = jnp.dot(a.astype(BF16), wdb_ref[...], preferred_element_type=F32) + bd_ref[0]

    @pl.when(act_ref[b] == 0)
    def _():
        obuf_ref[slot] = jnp.zeros(obuf_ref.shape[1:], F32)

    out_copy(b, slot).start()

    @pl.when(b == nb - 1)
    def _():
        out_copy(b - 1, 1 - slot).wait()
        out_copy(b, slot).wait()


def _experts(block_expert, block_active, block_new, xs, wg, bg, wu, bu, wd, bd):
    cap = xs.shape[0]
    n_blocks = cap // MOE_BLOCK
    assert n_blocks >= 2
    wspec = pl.BlockSpec((1, D_MODEL, D_MODEL), lambda b, be, act, new: (be[b], 0, 0))
    bspec = pl.BlockSpec((1, 1, D_MODEL), lambda b, be, act, new: (be[b], 0, 0))
    hbm = pl.BlockSpec(memory_space=pl.ANY)
    return pl.pallas_call(
        _expert_kernel,
        grid_spec=pltpu.PrefetchScalarGridSpec(
            num_scalar_prefetch=3, grid=(n_blocks,),
            in_specs=[hbm, wspec, bspec, wspec, bspec, wspec, bspec],
            out_specs=hbm,
            scratch_shapes=[pltpu.VMEM((D_MODEL, D_MODEL), BF16)] * 3
                           + [pltpu.VMEM((2, MOE_BLOCK, D_MODEL // 2), U32), pltpu.VMEM((2, MOE_BLOCK, D_MODEL), F32),
                              pltpu.SemaphoreType.DMA((2,)), pltpu.SemaphoreType.DMA((2,))]),
        out_shape=jax.ShapeDtypeStruct((cap, 1, D_MODEL), F32),
        compiler_params=_cparams("arbitrary"),
        name="moe_experts",
    )(block_expert, block_active, block_new, xs, wg, bg, wu, bu, wd, bd)


def _combine_kernel(hp_ref, hs_ref, gate_ref, dest_hbm, yb_hbm, g_ref, b_ref, yp_ref, ys_ref,
                    dest_smem, rows_ref, sem_idx, sem_rows, *, n_prompt):
    i = pl.program_id(0)
    n = TM_MOE * TOP_K
    slot = i % 2

    def gather_tile(tile, s):
        cp = pltpu.make_async_copy(dest_hbm.at[pl.ds(tile * n, n)], dest_smem.at[pl.ds(s * n, n)], sem_idx)
        cp.start()
        cp.wait()

        def issue(t, c):
            for k in range(TOP_K):
                pltpu.make_async_copy(yb_hbm.at[pl.ds(dest_smem[s * n + t * TOP_K + k], 1)],
                                      rows_ref.at[s, k, pl.ds(t, 1)], sem_rows.at[s]).start()
            return c

        lax.fori_loop(0, TM_MOE, issue, 0, unroll=2)

    @pl.when(i == 0)
    def _():
        gather_tile(0, 0)

    @pl.when(i + 1 < pl.num_programs(0))
    def _():
        gather_tile(i + 1, 1 - slot)

    pltpu.make_async_copy(rows_ref.at[slot], rows_ref.at[slot], sem_rows.at[slot]).wait()

    gate = gate_ref[...]
    ff = gate[:, 0:1] * rows_ref[slot, 0, :, 0, :]
    for k in range(1, TOP_K):
        ff = ff + gate[:, k:k + 1] * rows_ref[slot, k, :, 0, :]

    @pl.when(i < n_prompt)
    def _():
        yp_ref[...] = _layer_norm(DEEPNORM_ALPHA * hp_ref[...] + ff, g_ref[...], b_ref[...])

    @pl.when(i >= n_prompt)
    def _():
        ys_ref[...] = _layer_norm(DEEPNORM_ALPHA * hs_ref[...] + ff, g_ref[...], b_ref[...])


def _combine(h2_p, h2_s, gates, dest_flat, yb, ln_g, ln_b):
    n_prompt = h2_p.shape[0] // TM_MOE
    n_sample = h2_s.shape[0] // TM_MOE
    pmap = lambda i: (jnp.minimum(i, n_prompt - 1), 0)
    smap = lambda i: (jnp.maximum(i - n_prompt, 0), 0)
    vec = pl.BlockSpec((1, D_MODEL), lambda i: (0, 0))
    return pl.pallas_call(
        functools.partial(_combine_kernel, n_prompt=n_prompt),
        grid=(n_prompt + n_sample,),
        in_specs=[pl.BlockSpec((TM_MOE, D_MODEL), pmap), pl.BlockSpec((TM_MOE, D_MODEL), smap),
                  pl.BlockSpec((TM_MOE, TOP_K), lambda i: (i, 0)),
                  pl.BlockSpec(memory_space=pl.ANY), pl.BlockSpec(memory_space=pl.ANY), vec, vec],
        out_specs=[pl.BlockSpec((TM_MOE, D_MODEL), pmap), pl.BlockSpec((TM_MOE, D_MODEL), smap)],
        out_shape=[jax.ShapeDtypeStruct(h2_p.shape, F32), jax.ShapeDtypeStruct(h2_s.shape, F32)],
        scratch_shapes=[pltpu.SMEM((2 * TM_MOE * TOP_K,), I32),
                        pltpu.VMEM((2, TOP_K, TM_MOE, 1, D_MODEL), F32),
                        pltpu.SemaphoreType.DMA, pltpu.SemaphoreType.DMA((2,))],
        compiler_params=_cparams("arbitrary"),
        name="moe_combine",
    )(h2_p, h2_s, gates, dest_flat, yb, ln_g, ln_b)


def _tokens_major(a):
    n, k, tm = a.shape
    return jnp.transpose(a, (1, 0, 2)).reshape(k, n * tm)


def kernel(x_prompt, x_sample, mem_prompt, cache_diff_k, cache_diff_v, state_pool, cache_mem_k, cache_mem_v, w_in, lambda_q1, lambda_k1, lambda_q2, lambda_k2, subln_g, pool_w, pool_scale, w_out, ln1_g, ln1_b, xa_wq, xa_wk, xa_wv, xa_wo, ln2_g, ln2_b, router_w, router_b, moe_w_gate, moe_b_gate, moe_w_up, moe_b_up, moe_w_down, moe_b_down, ln3_g, ln3_b):
    assert w_in.shape[0] == DEPTH == 1
    batch, seq, d = x_prompt.shape
    dec_batch, dec_seq, _ = x_sample.shape
    past_len = cache_diff_k.shape[2]
    lambda_init = 0.8 - 0.6 * math.exp(-0.3 * 0)
    tp, ts = batch * seq, dec_batch * dec_seq

    vec = lambda a: a[0].reshape(1, -1)
    wts = dict(
        pool_w=pool_w[0].astype(BF16), pool_scale=vec(pool_scale), w_out=w_out[0].astype(BF16),
        ln1_g=vec(ln1_g), ln1_b=vec(ln1_b), xa_wq=xa_wq[0].astype(BF16), xa_wo=xa_wo[0].astype(BF16),
        ln2_g=vec(ln2_g), ln2_b=vec(ln2_b), router_wt=router_w[0].T.astype(BF16),
        router_b=router_b[0].reshape(N_EXPERTS, 1))
    w_in_bf = w_in[0].astype(BF16)
    lam_vecs = [vec(lambda_q1), vec(lambda_k1), vec(lambda_q2), vec(lambda_k2)]
    g_sub = vec(subln_g)

    xp2 = x_prompt.reshape(tp, d)
    qp, kp, vp, up, kpb, vpb = _inproj(xp2, w_in_bf, TM_PROJ)
    dp = _diff_prompt(qp, kpb, vpb, lam_vecs, g_sub, batch, seq, lambda_init)
    mk, mv, mkb, mvb = _memkv(mem_prompt.reshape(batch * N_MEM, d), xa_wk[0].astype(BF16),
                              xa_wv[0].astype(BF16), TM_PROJ)
    per16 = TM_POST // HIST_ROWS
    h2_p, h2p_p, idx_p, gate_p = _post(
        dp, up, up, lambda i: (jnp.maximum(i * per16 - 1, 0), 0), xp2, mkb, mvb, wts,
        tm=TM_POST, seq=seq, pos0=0, zero_first_hist=True)

    xs2 = x_sample.reshape(ts, d)
    qs, ks, vs, us, ksb, vsb = _inproj(xs2, w_in_bf, TM_PROJ)
    ds = _diff_sample(qs, cache_diff_k.reshape(dec_batch * past_len, 1, DIFF_WIDTH),
                      cache_diff_v.reshape(dec_batch * past_len, 1, DIFF_WIDTH), ksb, vsb, lam_vecs, g_sub,
                      dec_batch, dec_seq, lambda_init)
    hist_s = jnp.pad(state_pool[0], ((0, 0), (1, 0), (0, 0))).reshape(dec_batch * HIST_ROWS, POOL_WIDTH)
    h2_s, h2p_s, idx_s, gate_s = _post(
        ds, us, hist_s, lambda i: (i, 0), xs2,
        cache_mem_k[0].reshape(dec_batch * N_MEM, d).astype(BF16),
        cache_mem_v[0].reshape(dec_batch * N_MEM, d).astype(BF16), wts,
        tm=dec_seq, seq=dec_seq, pos0=past_len, zero_first_hist=False)

    t_all = tp + ts
    idx_all = jnp.concatenate([_tokens_major(idx_p), _tokens_major(idx_s)], axis=1)
    gate_all = jnp.concatenate([_tokens_major(gate_p), _tokens_major(gate_s)], axis=1)
    rank_all, counts = _ranks(idx_all)
    counts = counts[:, 0]
    padded = (counts + MOE_BLOCK - 1) // MOE_BLOCK * MOE_BLOCK
    pad_end = jnp.cumsum(padded)
    pad_start = (pad_end - padded).astype(I32)
    n_blocks = t_all * TOP_K // MOE_BLOCK + N_EXPERTS
    block_start = jnp.arange(n_blocks, dtype=I32) * MOE_BLOCK
    block_expert = jnp.minimum(jnp.sum(block_start[:, None] >= pad_end[None, :], axis=1), N_EXPERTS - 1).astype(I32)
    block_active = (block_start < pad_end[-1]).astype(I32)
    block_new = jnp.concatenate([jnp.ones((1,), I32), (block_expert[1:] != block_expert[:-1]).astype(I32)])
    dest_flat = _dests(pad_start, idx_all, rank_all).T.reshape(-1)
    xs = _dispatch(counts, pad_start, h2p_p, h2p_s, dest_flat, n_blocks * MOE_BLOCK)
    yb = _experts(block_expert, block_active, block_new, xs,
                  moe_w_gate[0], moe_b_gate[0].reshape(N_EXPERTS, 1, D_MODEL),
                  moe_w_up[0], moe_b_up[0].reshape(N_EXPERTS, 1, D_MODEL),
                  moe_w_down[0], moe_b_down[0].reshape(N_EXPERTS, 1, D_MODEL))
    y_p, y_s = _combine(h2_p, h2_s, gate_all.T, dest_flat, yb, vec(ln3_g), vec(ln3_b))

    heads = (N_DIFF_HEADS, DIFF_V_DIM)
    xa = (N_XA_HEADS, XA_HEAD_DIM)
    up3 = up.reshape(batch, seq, POOL_WIDTH)
    us3 = us.reshape(dec_batch, dec_seq, POOL_WIDTH)
    pool_s = jnp.concatenate([state_pool[0].astype(F32), us3], axis=1)[:, -POOL_HIST:]
    return (y_p.reshape(batch, seq, d), y_s.reshape(dec_batch, dec_seq, d),
            kp.reshape(1, batch, seq, *heads), vp.reshape(1, batch, seq, *heads),
            up3[:, seq - POOL_HIST:][None],
            mk.reshape(1, batch, N_MEM, *xa), mv.reshape(1, batch, N_MEM, *xa),
            ks.reshape(1, dec_batch, dec_seq, *heads), vs.reshape(1, dec_batch, dec_seq, *heads),
            pool_s[None])
```

```python
import functools
import math

import jax
import jax.numpy as jnp
from jax import lax
from jax.experimental import pallas as pl
from jax.experimental.pallas import tpu as pltpu

F32 = jnp.float32
BF16 = jnp.bfloat16
I32 = jnp.int32
U32 = jnp.uint32

D_MODEL = 1024
CHUNK = 64
N_DIFF_HEADS = 4
DIFF_HEAD_DIM = 64
DIFF_V_DIM = 2 * DIFF_HEAD_DIM
DIFF_WIDTH = N_DIFF_HEADS * DIFF_V_DIM
POOL_WIDTH = D_MODEL - DIFF_WIDTH
POOL_WINDOWS = (2, 4, 8, 16)
POOL_GROUP_DIM = POOL_WIDTH // len(POOL_WINDOWS)
POOL_HIST = max(POOL_WINDOWS) - 1
HIST_ROWS = POOL_HIST + 1
IN_WIDTH = 3 * DIFF_WIDTH + POOL_WIDTH
N_MEM = 256
N_XA_HEADS = 4
XA_HEAD_DIM = D_MODEL // N_XA_HEADS
N_EXPERTS = 32
TOP_K = 4
SWIGLU_LIMIT = 7.0
SWIGLU_ALPHA = 1.702
MOE_BLOCK = 256
LN_EPS = 1e-5
RMS_EPS = 1e-5
DEPTH = 1
DEEPNORM_ALPHA = (2.0 * DEPTH) ** 0.25
NEG_BIG = -1e30
LOG2_E = math.log2(math.e)

VMEM_LIMIT = 56 * 1024 * 1024

TM_PROJ = 512
TQ_ATT = 256
TK_ATT = 512
TK_PAST = 1024
TM_POST = 512
TR_RANK = 512
TM_MOE = 256


def _cparams(*sem):
    return pltpu.CompilerParams(dimension_semantics=sem, vmem_limit_bytes=VMEM_LIMIT)


def _layer_norm(x, g, b):
    mu = jnp.mean(x, axis=-1, keepdims=True)
    xc = x - mu
    var = jnp.mean(xc * xc, axis=-1, keepdims=True)
    return xc * lax.rsqrt(var + LN_EPS) * g + b


def _inproj_kernel(x_ref, w_ref, q_ref, k_hbm, v_hbm, u_ref, kb_ref, vb_ref, kv_ref, sem):
    i = pl.program_id(0)
    n = pl.num_programs(0)
    tm = x_ref.shape[0]
    slot = i % 2

    def cache_copy(step, s, which, dst_hbm):
        return pltpu.make_async_copy(kv_ref.at[s, which], dst_hbm.at[pl.ds(step * tm, tm), 0], sem.at[s, which])

    @pl.when(i >= 2)
    def _():
        cache_copy(i - 2, slot, 0, k_hbm).wait()
        cache_copy(i - 2, slot, 1, v_hbm).wait()

    x = x_ref[...].astype(BF16)
    w = DIFF_WIDTH
    q = jnp.dot(x, w_ref[:, 0:w], preferred_element_type=F32)
    q_ref[...] = (q * (DIFF_HEAD_DIM ** -0.5 * LOG2_E)).astype(BF16)
    k = jnp.dot(x, w_ref[:, w:2 * w], preferred_element_type=F32)
    kb_ref[...] = k.astype(BF16)
    v = jnp.dot(x, w_ref[:, 2 * w:3 * w], preferred_element_type=F32)
    vb_ref[...] = v.astype(BF16)
    kv_ref[slot, 0] = k
    kv_ref[slot, 1] = v
    cache_copy(i, slot, 0, k_hbm).start()
    cache_copy(i, slot, 1, v_hbm).start()
    u_ref[...] = jnp.dot(x, w_ref[:, 3 * w:], preferred_element_type=F32)

    @pl.when(i == n - 1)
    def _():
        @pl.when(n >= 2)
        def _():
            cache_copy(i - 1, 1 - slot, 0, k_hbm).wait()
            cache_copy(i - 1, 1 - slot, 1, v_hbm).wait()

        cache_copy(i, slot, 0, k_hbm).wait()
        cache_copy(i, slot, 1, v_hbm).wait()


def _inproj(x2d, w_in_bf, tm):
    t = x2d.shape[0]
    w = DIFF_WIDTH
    row = lambda i: (i, 0)
    out_spec = pl.BlockSpec((tm, w), row)
    head_spec = pl.BlockSpec(memory_space=pl.ANY)
    head_shape = jax.ShapeDtypeStruct((t, 1, w), F32)
    return pl.pallas_call(
        _inproj_kernel,
        grid=(t // tm,),
        in_specs=[pl.BlockSpec((tm, D_MODEL), row),
                  pl.BlockSpec((D_MODEL, IN_WIDTH), lambda i: (0, 0))],
        out_specs=[out_spec, head_spec, head_spec, out_spec, out_spec, out_spec],
        out_shape=[jax.ShapeDtypeStruct((t, w), BF16), head_shape, head_shape,
                   jax.ShapeDtypeStruct((t, w), F32),
                   jax.ShapeDtypeStruct((t, w), BF16), jax.ShapeDtypeStruct((t, w), BF16)],
        scratch_shapes=[pltpu.VMEM((2, 2, tm, w), F32), pltpu.SemaphoreType.DMA((2, 2))],
        compiler_params=_cparams("arbitrary"),
        name="inproj",
    )(x2d, w_in_bf)


def _memkv_kernel(x_ref, wk_ref, wv_ref, k_ref, v_ref, kb_ref, vb_ref):
    x = x_ref[...].astype(BF16)
    k = jnp.dot(x, wk_ref[...], preferred_element_type=F32)
    kb_ref[...] = k.astype(BF16)
    v = jnp.dot(x, wv_ref[...], preferred_element_type=F32)
    vb_ref[...] = v.astype(BF16)
    for h in range(N_XA_HEADS):
        cs = slice(h * XA_HEAD_DIM, (h + 1) * XA_HEAD_DIM)
        k_ref[:, h, :] = k[:, cs]
        v_ref[:, h, :] = v[:, cs]


def _memkv(mem2d, wk_bf, wv_bf, tm):
    t = mem2d.shape[0]
    row = lambda i: (i, 0)
    full = lambda i: (0, 0)
    spec = pl.BlockSpec((tm, D_MODEL), row)
    head_spec = pl.BlockSpec((tm, N_XA_HEADS, XA_HEAD_DIM), lambda i: (i, 0, 0))
    head_shape = jax.ShapeDtypeStruct((t, N_XA_HEADS, XA_HEAD_DIM), F32)
    return pl.pallas_call(
        _memkv_kernel,
        grid=(t // tm,),
        in_specs=[spec, pl.BlockSpec((D_MODEL, D_MODEL), full), pl.BlockSpec((D_MODEL, D_MODEL), full)],
        out_specs=[head_spec, head_spec, spec, spec],
        out_shape=[head_shape, head_shape,
                   jax.ShapeDtypeStruct((t, D_MODEL), BF16), jax.ShapeDtypeStruct((t, D_MODEL), BF16)],
        compiler_params=_cparams("parallel"),
        name="memkv",
    )(mem2d, wk_bf, wv_bf)


def _diff_lambda(lq1_ref, lk1_ref, lq2_ref, lk2_ref, lambda_init):
    s1 = jnp.sum(lq1_ref[...] * lk1_ref[...], axis=-1, keepdims=True)
    s2 = jnp.sum(lq2_ref[...] * lk2_ref[...], axis=-1, keepdims=True)
    return jnp.exp(s1) - jnp.exp(s2) + lambda_init


def _stack_maps(qh):
    lane = lax.broadcasted_iota(I32, qh.shape, 1)
    zero = jnp.zeros_like(qh)
    return jnp.concatenate([jnp.where(lane < DIFF_HEAD_DIM, qh, zero),
                            jnp.where(lane >= DIFF_HEAD_DIM, qh, zero)], axis=0)


def _lanes(x, n):
    w = x.shape[1]
    return x[:, :n] if n <= w else jnp.tile(x, (1, n // w))


def _softmax_step(qq, kh, vh, m_ref, acc_ref, mask):
    s = lax.dot_general(qq, kh, (((1,), (1,)), ((), ())), preferred_element_type=F32)
    if mask is not None:
        s = jnp.where(mask, s, NEG_BIG)
    m_old = m_ref[...]
    m_new = jnp.maximum(m_old, jnp.max(s, axis=-1, keepdims=True))
    p = jnp.exp2(s - _lanes(m_new, s.shape[1])).astype(BF16)
    ones_col = (lax.broadcasted_iota(I32, vh.shape, 1) == 0).astype(BF16)
    pv = jnp.dot(p, jnp.concatenate([vh, ones_col], axis=1), preferred_element_type=F32)
    a = jnp.exp2(m_old - m_new)
    acc_ref[...] = _lanes(a, pv.shape[1]) * acc_ref[...] + pv
    m_ref[...] = m_new


def _diff_finish(acc, lam, g, lambda_init, tq):
    hd = DIFF_V_DIM
    o = acc[:tq, :hd] / acc[:tq, hd:hd + 1] - lam * (acc[tq:, :hd] / acc[tq:, hd:hd + 1])
    o = o * lax.rsqrt(jnp.mean(o * o, axis=-1, keepdims=True) + RMS_EPS)
    return o * g * (1.0 - lambda_init)


def _diff_prompt_kernel(q_ref, k_ref, v_ref, lq1_ref, lk1_ref, lq2_ref, lk2_ref, g_ref, o_ref, m_ref, acc_ref,
                        *, lambda_init):
    i = pl.program_id(1)
    tq, tk, hd = TQ_ATT, TK_ATT, DIFF_V_DIM
    n_full = (i * tq) // tk
    qrow = lax.broadcasted_iota(I32, (2 * tq, tk), 0)
    qrow = jnp.where(qrow >= tq, qrow - tq, qrow) + i * tq
    kcol = lax.broadcasted_iota(I32, (2 * tq, tk), 1) + n_full * tk
    last_mask = (kcol // CHUNK) <= (qrow // CHUNK)
    m_ref[...] = jnp.full(m_ref.shape, NEG_BIG, F32)
    acc_ref[...] = jnp.zeros(acc_ref.shape, F32)

    def step(j, mask):
        r0 = pl.multiple_of(j * tk, tk)
        for h in range(N_DIFF_HEADS):
            cs = slice(h * hd, (h + 1) * hd)
            _softmax_step(_stack_maps(q_ref[:, cs]), k_ref[pl.ds(r0, tk), cs], v_ref[pl.ds(r0, tk), cs],
                          m_ref.at[h], acc_ref.at[h], mask)

    def body(j, c):
        step(j, None)
        return c

    lax.fori_loop(0, n_full, body, 0)
    step(n_full, last_mask)
    lam = _diff_lambda(lq1_ref, lk1_ref, lq2_ref, lk2_ref, lambda_init)
    for h in range(N_DIFF_HEADS):
        o_ref[:, h * hd:(h + 1) * hd] = _diff_finish(acc_ref[h], lam, g_ref[...], lambda_init,
                                                     tq).astype(o_ref.dtype)


def _diff_prompt(qb, kb, vb, lam_vecs, subln_g, batch, seq, lambda_init):
    assert TQ_ATT % CHUNK == 0 and TK_ATT % TQ_ATT == 0 and seq % TK_ATT == 0
    nq = seq // TQ_ATT
    w = DIFF_WIDTH
    small = lambda b, i: (0, 0)
    return pl.pallas_call(
        functools.partial(_diff_prompt_kernel, lambda_init=lambda_init),
        grid=(batch, nq),
        in_specs=[pl.BlockSpec((TQ_ATT, w), lambda b, i: (b * nq + i, 0)),
                  pl.BlockSpec((seq, w), lambda b, i: (b, 0)),
                  pl.BlockSpec((seq, w), lambda b, i: (b, 0))]
                 + [pl.BlockSpec((1, DIFF_HEAD_DIM), small)] * 4
                 + [pl.BlockSpec((1, DIFF_V_DIM), small)],
        out_specs=pl.BlockSpec((TQ_ATT, w), lambda b, i: (b * nq + i, 0)),
        out_shape=jax.ShapeDtypeStruct((batch * seq, w), BF16),
        scratch_shapes=[pltpu.VMEM((N_DIFF_HEADS, 2 * TQ_ATT, 128), F32),
                        pltpu.VMEM((N_DIFF_HEADS, 2 * TQ_ATT, 2 * DIFF_V_DIM), F32)],
        compiler_params=_cparams("parallel", "arbitrary"),
        name="diff_attn_prompt",
    )(qb, kb, vb, *lam_vecs, subln_g)


def _diff_sample_kernel(q_ref, pk_hbm, pv_hbm, nk_ref, nv_ref, lq1_ref, lk1_ref, lq2_ref, lk2_ref, g_ref,
                        o_ref, m_ref, acc_ref, kv_ref, sem, *, lambda_init, n_past):
    b = pl.program_id(0)
    j = pl.program_id(1)
    tq, hd = q_ref.shape[0], DIFF_V_DIM
    tk = kv_ref.shape[2]
    n_tiles = pl.num_programs(0) * n_past

    def past_copy(tile, which, src_hbm):
        s = tile % 2
        return pltpu.make_async_copy(src_hbm.at[pl.ds(tile * tk, tk), 0], kv_ref.at[s, which], sem.at[s, which])

    def fetch(tile):
        past_copy(tile, 0, pk_hbm).start()
        past_copy(tile, 1, pv_hbm).start()

    @pl.when(jnp.logical_and(b == 0, j == 0))
    def _():
        fetch(0)

    @pl.when(j == 0)
    def _():
        m_ref[...] = jnp.full(m_ref.shape, NEG_BIG, F32)
        acc_ref[...] = jnp.zeros(acc_ref.shape, F32)

    def update(k_of, v_of):
        for h in range(N_DIFF_HEADS):
            qq = _stack_maps(q_ref[:, h * hd:(h + 1) * hd])
            _softmax_step(qq, k_of(h), v_of(h), m_ref.at[h], acc_ref.at[h], None)

    @pl.when(j < n_past)
    def _():
        cur = b * n_past + j

        @pl.when(cur + 1 < n_tiles)
        def _():
            fetch(cur + 1)

        past_copy(cur, 0, pk_hbm).wait()
        past_copy(cur, 1, pv_hbm).wait()
        s = cur % 2
        update(lambda h: kv_ref[s, 0, :, h * hd:(h + 1) * hd].astype(BF16),
               lambda h: kv_ref[s, 1, :, h * hd:(h + 1) * hd].astype(BF16))

    @pl.when(j == n_past)
    def _():
        update(lambda h: nk_ref[:, h * hd:(h + 1) * hd], lambda h: nv_ref[:, h * hd:(h + 1) * hd])
        lam = _diff_lambda(lq1_ref, lk1_ref, lq2_ref, lk2_ref, lambda_init)
        for h in range(N_DIFF_HEADS):
            o_ref[:, h * hd:(h + 1) * hd] = _diff_finish(acc_ref[h], lam, g_ref[...], lambda_init,
                                                         tq).astype(o_ref.dtype)


def _diff_sample(qb, past_k, past_v, kb, vb, lam_vecs, subln_g, batch, seq, lambda_init):
    past_len = past_k.shape[0] // batch
    assert past_len % TK_PAST == 0
    n_past = past_len // TK_PAST
    w = DIFF_WIDTH
    small = lambda b, j: (0, 0)
    hbm = pl.BlockSpec(memory_space=pl.ANY)
    new_spec = pl.BlockSpec((seq, w), lambda b, j: (b, 0))
    return pl.pallas_call(
        functools.partial(_diff_sample_kernel, lambda_init=lambda_init, n_past=n_past),
        grid=(batch, n_past + 1),
        in_specs=[new_spec, hbm, hbm, new_spec, new_spec]
                 + [pl.BlockSpec((1, DIFF_HEAD_DIM), small)] * 4
                 + [pl.BlockSpec((1, DIFF_V_DIM), small)],
        out_specs=new_spec,
        out_shape=jax.ShapeDtypeStruct((batch * seq, w), BF16),
        scratch_shapes=[pltpu.VMEM((N_DIFF_HEADS, 2 * seq, 128), F32),
                        pltpu.VMEM((N_DIFF_HEADS, 2 * seq, 2 * DIFF_V_DIM), F32),
                        pltpu.VMEM((2, 2, TK_PAST, w), F32), pltpu.SemaphoreType.DMA((2, 2))],
        compiler_params=_cparams("arbitrary", "arbitrary"),
        name="diff_attn_sample",
    )(qb, past_k, past_v, kb, vb, *lam_vecs, subln_g)


def _post_kernel(d_ref, u_ref, hist_ref, x_ref, pw_ref, ps_ref, wo1_ref, g1_ref, b1_ref, wq_ref, mk_ref,
                 mv_ref, wo2_ref, g2_ref, b2_ref, rw_ref, rb_ref,
                 h2_ref, h2p_ref, idx_ref, gate_ref, full_ref, tmp_ref, *, tm, seq, pos0, zero_first_hist):
    i = pl.program_id(0)
    row0 = (i * tm) % seq

    hist = hist_ref[...]
    if zero_first_hist:
        hist = jnp.where(row0 == 0, jnp.zeros_like(hist), hist)
    full_ref[0:HIST_ROWS, :] = hist
    u = u_ref[...]
    full_ref[HIST_ROWS:, :] = u
    pos = pos0 + row0 + lax.broadcasted_iota(I32, (tm, 1), 0)
    n_end = HIST_ROWS + tm

    def trailing_sum(cs, w):
        read = lambda a, b: full_ref[a:b, cs]
        lo, sh, level = HIST_ROWS - (w - 2), 1, 0
        while True:
            out = read(lo, n_end) + read(lo - sh, n_end - sh)
            sh *= 2
            if sh == w:
                return out
            tmp_ref[level % 2, 0:n_end - lo, :] = out
            read = lambda a, b, base=lo, buf=level % 2: tmp_ref[buf, a - base:b - base, :]
            lo, level = lo + sh, level + 1

    pooled = []
    for gi, w in enumerate(POOL_WINDOWS):
        cs = slice(gi * POOL_GROUP_DIM, (gi + 1) * POOL_GROUP_DIM)
        inv_cnt = 1.0 / jnp.minimum(w, pos + 1).astype(F32)
        m = trailing_sum(cs, w) * inv_cnt - u[:, cs]
        y = jnp.dot(m.astype(BF16), pw_ref[gi], preferred_element_type=F32)
        pooled.append((y * ps_ref[:, cs]).astype(BF16))
    mixed_in = jnp.concatenate([d_ref[...]] + pooled, axis=-1)

    mix = jnp.dot(mixed_in, wo1_ref[...], preferred_element_type=F32)
    h1 = _layer_norm(DEEPNORM_ALPHA * x_ref[...] + mix, g1_ref[...], b1_ref[...])

    q = jnp.dot(h1.astype(BF16), wq_ref[...], preferred_element_type=F32)
    qb = (q * (XA_HEAD_DIM ** -0.5)).astype(BF16)
    heads = []
    for h in range(N_XA_HEADS):
        cs = slice(h * XA_HEAD_DIM, (h + 1) * XA_HEAD_DIM)
        s = lax.dot_general(qb[:, cs], mk_ref[:, cs], (((1,), (1,)), ((), ())), preferred_element_type=F32)
        e = jnp.exp(s - jnp.max(s, axis=-1, keepdims=True))
        p = e / jnp.sum(e, axis=-1, keepdims=True)
        heads.append(jnp.dot(p.astype(BF16), mv_ref[:, cs], preferred_element_type=F32).astype(BF16))
    ca = jnp.dot(jnp.concatenate(heads, axis=-1), wo2_ref[...], preferred_element_type=F32)
    h2 = _layer_norm(DEEPNORM_ALPHA * h1 + ca, g2_ref[...], b2_ref[...])
    h2_ref[...] = h2

    hb = h2.astype(BF16)
    bits = pltpu.bitcast(hb.astype(F32), U32)
    half = D_MODEL // 2
    h2p_ref[...] = (bits[:, :half] >> 16) | (bits[:, half:] & jnp.uint32(0xFFFF0000))

    logits = lax.dot_general(rw_ref[...], hb, (((1,), (1,)), ((), ())), preferred_element_type=F32) + rb_ref[...]
    erow = lax.broadcasted_iota(I32, logits.shape, 0)
    vals, idxs = [], []
    for _ in range(TOP_K):
        mx = jnp.max(logits, axis=0, keepdims=True)
        ix = jnp.min(jnp.where(logits == mx, erow, N_EXPERTS), axis=0, keepdims=True)
        vals.append(mx)
        idxs.append(ix)
        logits = jnp.where(erow == ix, -jnp.inf, logits)
    ex = [jnp.exp(v - vals[0]) for v in vals]
    den = ex[0] + ex[1] + ex[2] + ex[3]
    idx_ref[0] = jnp.concatenate(idxs, axis=0)
    gate_ref[0] = jnp.concatenate([e / den for e in ex], axis=0)


def _post(diff_out, u, hist_arr, hist_map, x2d, mkb, mvb, wts, *, tm, seq, pos0, zero_first_hist):
    t = x2d.shape[0]
    n = t // tm
    assert seq % tm == 0 or tm % seq == 0
    row = lambda i: (i, 0)
    full2 = lambda i: (0, 0)
    mem_map = lambda i: ((i * tm) // seq, 0)
    vec = pl.BlockSpec((1, D_MODEL), full2)
    wspec = pl.BlockSpec((D_MODEL, D_MODEL), full2)
    in_specs = [
        pl.BlockSpec((tm, DIFF_WIDTH), row), pl.BlockSpec((tm, POOL_WIDTH), row),
        pl.BlockSpec((HIST_ROWS, POOL_WIDTH), hist_map), pl.BlockSpec((tm, D_MODEL), row),
        pl.BlockSpec((len(POOL_WINDOWS), POOL_GROUP_DIM, POOL_GROUP_DIM), lambda i: (0, 0, 0)),
        pl.BlockSpec((1, POOL_WIDTH), full2),
        wspec, vec, vec,
        wspec, pl.BlockSpec((N_MEM, D_MODEL), mem_map), pl.BlockSpec((N_MEM, D_MODEL), mem_map),
        wspec, vec, vec,
        pl.BlockSpec((N_EXPERTS, D_MODEL), full2), pl.BlockSpec((N_EXPERTS, 1), full2),
    ]
    out_specs = [pl.BlockSpec((tm, D_MODEL), row), pl.BlockSpec((tm, D_MODEL // 2), row),
                 pl.BlockSpec((1, TOP_K, tm), lambda i: (i, 0, 0)),
                 pl.BlockSpec((1, TOP_K, tm), lambda i: (i, 0, 0))]
    out_shape = [jax.ShapeDtypeStruct((t, D_MODEL), F32), jax.ShapeDtypeStruct((t, D_MODEL // 2), U32),
                 jax.ShapeDtypeStruct((n, TOP_K, tm), I32), jax.ShapeDtypeStruct((n, TOP_K, tm), F32)]
    return pl.pallas_call(
        functools.partial(_post_kernel, tm=tm, seq=seq, pos0=pos0, zero_first_hist=zero_first_hist),
        grid=(n,),
        in_specs=in_specs,
        out_specs=out_specs,
        out_shape=out_shape,
        scratch_shapes=[pltpu.VMEM((HIST_ROWS + tm, POOL_WIDTH), F32),
                        pltpu.VMEM((2, HIST_ROWS + tm, POOL_GROUP_DIM), F32)],
        compiler_params=_cparams("parallel"),
        name="post_attn",
    )(diff_out, u, hist_arr, x2d, wts["pool_w"], wts["pool_scale"], wts["w_out"], wts["ln1_g"], wts["ln1_b"],
      wts["xa_wq"], mkb, mvb, wts["xa_wo"], wts["ln2_g"], wts["ln2_b"], wts["router_wt"], wts["router_b"])


def _rank_kernel(idx_ref, rank_ref, cnt_ref, carry_ref):
    @pl.when(pl.program_id(0) == 0)
    def _():
        carry_ref[...] = jnp.zeros(carry_ref.shape, F32)

    idx = idx_ref[...]
    tr = idx.shape[1]
    erow = lax.broadcasted_iota(I32, (N_EXPERTS, tr), 0)
    hits = [erow == idx[k:k + 1, :] for k in range(TOP_K)]
    onehot = sum(h.astype(F32) for h in hits)
    earlier = (lax.broadcasted_iota(I32, (tr, tr), 0) < lax.broadcasted_iota(I32, (tr, tr), 1)).astype(BF16)
    before = jnp.dot(onehot.astype(BF16), earlier, preferred_element_type=F32) + carry_ref[:, 0:1]
    ranks = [jnp.sum(jnp.where(h, before, 0.0), axis=0, keepdims=True) for h in hits]
    rank_ref[...] = jnp.concatenate(ranks, axis=0).astype(I32)
    carry_ref[...] = carry_ref[...] + jnp.sum(onehot, axis=1, keepdims=True)
    cnt_ref[...] = carry_ref[...].astype(I32)


def _ranks(idx_all):
    t = idx_all.shape[1]
    assert t % TR_RANK == 0
    return pl.pallas_call(
        _rank_kernel,
        grid=(t // TR_RANK,),
        in_specs=[pl.BlockSpec((TOP_K, TR_RANK), lambda i: (0, i))],
        out_specs=[pl.BlockSpec((TOP_K, TR_RANK), lambda i: (0, i)),
                   pl.BlockSpec((N_EXPERTS, 128), lambda i: (0, 0))],
        out_shape=[jax.ShapeDtypeStruct((TOP_K, t), I32), jax.ShapeDtypeStruct((N_EXPERTS, 128), I32)],
        scratch_shapes=[pltpu.VMEM((N_EXPERTS, 128), F32)],
        compiler_params=_cparams("arbitrary"),
        name="moe_rank",
    )(idx_all)


def _dest_kernel(start_ref, idx_ref, rank_ref, dest_ref):
    idx = idx_ref[...]
    base = jnp.zeros(idx.shape, I32)
    for e in range(N_EXPERTS):
        base = jnp.where(idx == e, start_ref[e], base)
    dest_ref[...] = base + rank_ref[...]


def _dests(pad_start, idx_all, rank_all):
    t = idx_all.shape[1]
    spec = pl.BlockSpec((TOP_K, TR_RANK), lambda i, s: (0, i))
    return pl.pallas_call(
        _dest_kernel,
        grid_spec=pltpu.PrefetchScalarGridSpec(
            num_scalar_prefetch=1, grid=(t // TR_RANK,), in_specs=[spec, spec], out_specs=spec),
        out_shape=jax.ShapeDtypeStruct((TOP_K, t), I32),
        compiler_params=_cparams("parallel"),
        name="moe_dest",
    )(pad_start, idx_all, rank_all)


def _row_copy(src_ref, src_row, dst_ref, dst_row, sem):
    return pltpu.make_async_copy(src_ref.at[pl.ds(src_row, 1)], dst_ref.at[pl.ds(dst_row, 1), 0], sem)


def _dispatch_kernel(cnt_ref, start_ref, hp_ref, hs_ref, dest_hbm, xs_out, dest_smem, zero_ref,
                     sem_idx, sem_rows, sem_fill, *, n_prompt):
    i = pl.program_id(0)
    n = TM_MOE * TOP_K
    cp = pltpu.make_async_copy(dest_hbm.at[pl.ds(i * n, n)], dest_smem, sem_idx)
    cp.start()

    @pl.when(i == 0)
    def _():
        zero_ref[...] = jnp.zeros(zero_ref.shape, U32)
        for e in range(N_EXPERTS):
            n_pad = (cnt_ref[e] + MOE_BLOCK - 1) // MOE_BLOCK * MOE_BLOCK - cnt_ref[e]
            first = start_ref[e] + cnt_ref[e]

            def fill(r, c):
                _row_copy(zero_ref, 0, xs_out, first + r, sem_fill).start()
                return c

            lax.fori_loop(0, n_pad, fill, 0)

            def drain_fill(r, c):
                _row_copy(zero_ref, 0, xs_out, 0, sem_fill).wait()
                return c

            lax.fori_loop(0, n_pad, drain_fill, 0)

        last = N_EXPERTS - 1
        n_used = (start_ref[last] + cnt_ref[last] + MOE_BLOCK - 1) // MOE_BLOCK
        n_blocks = xs_out.shape[0] // MOE_BLOCK

        def block_copy(b):
            return pltpu.make_async_copy(zero_ref, xs_out.at[pl.ds(b * MOE_BLOCK, MOE_BLOCK), 0], sem_fill)

        def fill_block(b, c):
            block_copy(b).start()
            return c

        lax.fori_loop(n_used, n_blocks, fill_block, 0)

        def drain_block(b, c):
            block_copy(b).wait()
            return c

        lax.fori_loop(n_used, n_blocks, drain_block, 0)

    cp.wait()

    def scatter_from(src_ref):
        def issue(t, c):
            for k in range(TOP_K):
                _row_copy(src_ref, t, xs_out, dest_smem[t * TOP_K + k], sem_rows).start(priority=k % 2)
            return c

        lax.fori_loop(0, TM_MOE, issue, 0, unroll=2)

    @pl.when(i < n_prompt)
    def _():
        scatter_from(hp_ref)

    @pl.when(i >= n_prompt)
    def _():
        scatter_from(hs_ref)

    tile_rows = xs_out.at[pl.ds(0, n)]
    pltpu.make_async_copy(tile_rows, tile_rows, sem_rows).wait()


def _dispatch(counts, pad_start, h2p_p, h2p_s, dest_flat, cap):
    n_prompt = h2p_p.shape[0] // TM_MOE
    n_sample = h2p_s.shape[0] // TM_MOE
    half = D_MODEL // 2
    hbm = pl.BlockSpec(memory_space=pl.ANY)
    return pl.pallas_call(
        functools.partial(_dispatch_kernel, n_prompt=n_prompt),
        grid_spec=pltpu.PrefetchScalarGridSpec(
            num_scalar_prefetch=2, grid=(n_prompt + n_sample,),
            in_specs=[pl.BlockSpec((TM_MOE, half), lambda i, c, s: (jnp.minimum(i, n_prompt - 1), 0)),
                      pl.BlockSpec((TM_MOE, half), lambda i, c, s: (jnp.maximum(i - n_prompt, 0), 0)),
                      hbm],
            out_specs=hbm,
            scratch_shapes=[pltpu.SMEM((TM_MOE * TOP_K,), I32), pltpu.VMEM((MOE_BLOCK, half), U32),
                            pltpu.SemaphoreType.DMA, pltpu.SemaphoreType.DMA, pltpu.SemaphoreType.DMA]),
        out_shape=jax.ShapeDtypeStruct((cap, 1, half), U32),
        compiler_params=_cparams("arbitrary"),
        name="moe_dispatch",
    )(counts, pad_start, h2p_p, h2p_s, dest_flat)


def _expert_kernel(be_ref, act_ref, new_ref, xs_hbm, wg_ref, bg_ref, wu_ref, bu_ref, wd_ref, bd_ref, y_hbm,
                   wgb_ref, wub_ref, wdb_ref, xbuf_ref, obuf_ref, sem_in, sem_out):
    b = pl.program_id(0)
    nb = pl.num_programs(0)
    slot = b % 2

    def in_copy(blk, s):
        return pltpu.make_async_copy(xs_hbm.at[pl.ds(blk * MOE_BLOCK, MOE_BLOCK), 0], xbuf_ref.at[s], sem_in.at[s])

    def out_copy(blk, s):
        return pltpu.make_async_copy(obuf_ref.at[s], y_hbm.at[pl.ds(blk * MOE_BLOCK, MOE_BLOCK), 0], sem_out.at[s])

    @pl.when(b == 0)
    def _():
        in_copy(0, 0).start()

    nxt = jnp.minimum(b + 1, nb - 1)

    @pl.when(jnp.logical_and(b + 1 < nb, act_ref[nxt] == 1))
    def _():
        in_copy(b + 1, 1 - slot).start()

    @pl.when(b >= 2)
    def _():
        out_copy(b - 2, slot).wait()

    @pl.when(new_ref[b] == 1)
    def _():
        wgb_ref[...] = wg_ref[0].astype(BF16)
        wub_ref[...] = wu_ref[0].astype(BF16)
        wdb_ref[...] = wd_ref[0].astype(BF16)

    @pl.when(act_ref[b] == 1)
    def _():
        in_copy(b, slot).wait()
        p = xbuf_ref[slot]
        lo = pltpu.bitcast(p << 16, F32)
        hi = pltpu.bitcast(p & jnp.uint32(0xFFFF0000), F32)
        x = jnp.concatenate([lo, hi], axis=-1).astype(BF16)
        g = jnp.dot(x, wgb_ref[...], preferred_element_type=F32) + bg_ref[0]
        u = jnp.dot(x, wub_ref[...], preferred_element_type=F32) + bu_ref[0]
        g = jnp.minimum(g, SWIGLU_LIMIT)
        u = jnp.clip(u, -SWIGLU_LIMIT, SWIGLU_LIMIT)
        a = (u + 1.0) * (g * jax.nn.sigmoid(SWIGLU_ALPHA * g))
        obuf_ref[slot] = jnp.dot(a.astype(BF16), wdb_ref[...], preferred_element_type=F32) + bd_ref[0]

    @pl.when(act_ref[b] == 0)
    def _():
        obuf_ref[slot] = jnp.zeros(obuf_ref.shape[1:], F32)

    out_copy(b, slot).start()

    @pl.when(b == nb - 1)
    def _():
        out_copy(b - 1, 1 - slot).wait()
        out_copy(b, slot).wait()


def _experts(block_expert, block_active, block_new, xs, wg, bg, wu, bu, wd, bd):
    cap = xs.shape[0]
    n_blocks = cap // MOE_BLOCK
    assert n_blocks >= 2
    wspec = pl.BlockSpec((1, D_MODEL, D_MODEL), lambda b, be, act, new: (be[b], 0, 0))
    bspec = pl.BlockSpec((1, 1, D_MODEL), lambda b, be, act, new: (be[b], 0, 0))
    hbm = pl.BlockSpec(memory_space=pl.ANY)
    return pl.pallas_call(
        _expert_kernel,
        grid_spec=pltpu.PrefetchScalarGridSpec(
            num_scalar_prefetch=3, grid=(n_blocks,),
            in_specs=[hbm, wspec, bspec, wspec, bspec, wspec, bspec],
            out_specs=hbm,
            scratch_shapes=[pltpu.VMEM((D_MODEL, D_MODEL), BF16)] * 3
                           + [pltpu.VMEM((2, MOE_BLOCK, D_MODEL // 2), U32), pltpu.VMEM((2, MOE_BLOCK, D_MODEL), F32),
                              pltpu.SemaphoreType.DMA((2,)), pltpu.SemaphoreType.DMA((2,))]),
        out_shape=jax.ShapeDtypeStruct((cap, 1, D_MODEL), F32),
        compiler_params=_cparams("arbitrary"),
        name="moe_experts",
    )(block_expert, block_active, block_new, xs, wg, bg, wu, bu, wd, bd)


def _combine_kernel(hp_ref, hs_ref, gate_ref, dest_hbm, yb_hbm, g_ref, b_ref, yp_ref, ys_ref,
                    dest_smem, rows_ref, sem_idx, sem_rows, *, n_prompt):
    i = pl.program_id(0)
    n = TM_MOE * TOP_K
    slot = i % 2

    def gather_tile(tile, s):
        cp = pltpu.make_async_copy(dest_hbm.at[pl.ds(tile * n, n)], dest_smem.at[pl.ds(s * n, n)], sem_idx)
        cp.start()
        cp.wait()

        def issue(t, c):
            for k in range(TOP_K):
                pltpu.make_async_copy(yb_hbm.at[pl.ds(dest_smem[s * n + t * TOP_K + k], 1)],
                                      rows_ref.at[s, k, pl.ds(t, 1)], sem_rows.at[s]).start(priority=k % 2)
            return c

        lax.fori_loop(0, TM_MOE, issue, 0, unroll=2)

    @pl.when(i == 0)
    def _():
        gather_tile(0, 0)

    @pl.when(i + 1 < pl.num_programs(0))
    def _():
        gather_tile(i + 1, 1 - slot)

    pltpu.make_async_copy(rows_ref.at[slot], rows_ref.at[slot], sem_rows.at[slot]).wait()

    gate = gate_ref[...]
    ff = gate[:, 0:1] * rows_ref[slot, 0, :, 0, :]
    for k in range(1, TOP_K):
        ff = ff + gate[:, k:k + 1] * rows_ref[slot, k, :, 0, :]

    @pl.when(i < n_prompt)
    def _():
        yp_ref[...] = _layer_norm(DEEPNORM_ALPHA * hp_ref[...] + ff, g_ref[...], b_ref[...])

    @pl.when(i >= n_prompt)
    def _():
        ys_ref[...] = _layer_norm(DEEPNORM_ALPHA * hs_ref[...] + ff, g_ref[...], b_ref[...])


def _combine(h2_p, h2_s, gates, dest_flat, yb, ln_g, ln_b):
    n_prompt = h2_p.shape[0] // TM_MOE
    n_sample = h2_s.shape[0] // TM_MOE
    pmap = lambda i: (jnp.minimum(i, n_prompt - 1), 0)
    smap = lambda i: (jnp.maximum(i - n_prompt, 0), 0)
    vec = pl.BlockSpec((1, D_MODEL), lambda i: (0, 0))
    return pl.pallas_call(
        functools.partial(_combine_kernel, n_prompt=n_prompt),
        grid=(n_prompt + n_sample,),
        in_specs=[pl.BlockSpec((TM_MOE, D_MODEL), pmap), pl.BlockSpec((TM_MOE, D_MODEL), smap),
                  pl.BlockSpec((TM_MOE, TOP_K), lambda i: (i, 0)),
                  pl.BlockSpec(memory_space=pl.ANY), pl.BlockSpec(memory_space=pl.ANY), vec, vec],
        out_specs=[pl.BlockSpec((TM_MOE, D_MODEL), pmap), pl.BlockSpec((TM_MOE, D_MODEL), smap)],
        out_shape=[jax.ShapeDtypeStruct(h2_p.shape, F32), jax.ShapeDtypeStruct(h2_s.shape, F32)],
        scratch_shapes=[pltpu.SMEM((2 * TM_MOE * TOP_K,), I32),
                        pltpu.VMEM((2, TOP_K, TM_MOE, 1, D_MODEL), F32),
                        pltpu.SemaphoreType.DMA, pltpu.SemaphoreType.DMA((2,))],
        compiler_params=_cparams("arbitrary"),
        name="moe_combine",
    )(h2_p, h2_s, gates, dest_flat, yb, ln_g, ln_b)


def _tokens_major(a):
    n, k, tm = a.shape
    return jnp.transpose(a, (1, 0, 2)).reshape(k, n * tm)


def kernel(x_prompt, x_sample, mem_prompt, cache_diff_k, cache_diff_v, state_pool, cache_mem_k, cache_mem_v, w_in, lambda_q1, lambda_k1, lambda_q2, lambda_k2, subln_g, pool_w, pool_scale, w_out, ln1_g, ln1_b, xa_wq, xa_wk, xa_wv, xa_wo, ln2_g, ln2_b, router_w, router_b, moe_w_gate, moe_b_gate, moe_w_up, moe_b_up, moe_w_down, moe_b_down, ln3_g, ln3_b):
    assert w_in.shape[0] == DEPTH == 1
    batch, seq, d = x_prompt.shape
    dec_batch, dec_seq, _ = x_sample.shape
    past_len = cache_diff_k.shape[2]
    lambda_init = 0.8 - 0.6 * math.exp(-0.3 * 0)
    tp, ts = batch * seq, dec_batch * dec_seq

    vec = lambda a: a[0].reshape(1, -1)
    wts = dict(
        pool_w=pool_w[0].astype(BF16), pool_scale=vec(pool_scale), w_out=w_out[0].astype(BF16),
        ln1_g=vec(ln1_g), ln1_b=vec(ln1_b), xa_wq=xa_wq[0].astype(BF16), xa_wo=xa_wo[0].astype(BF16),
        ln2_g=vec(ln2_g), ln2_b=vec(ln2_b), router_wt=router_w[0].T.astype(BF16),
        router_b=router_b[0].reshape(N_EXPERTS, 1))
    w_in_bf = w_in[0].astype(BF16)
    lam_vecs = [vec(lambda_q1), vec(lambda_k1), vec(lambda_q2), vec(lambda_k2)]
    g_sub = vec(subln_g)

    xp2 = x_prompt.reshape(tp, d)
    qp, kp, vp, up, kpb, vpb = _inproj(xp2, w_in_bf, TM_PROJ)
    dp = _diff_prompt(qp, kpb, vpb, lam_vecs, g_sub, batch, seq, lambda_init)
    mk, mv, mkb, mvb = _memkv(mem_prompt.reshape(batch * N_MEM, d), xa_wk[0].astype(BF16),
                              xa_wv[0].astype(BF16), TM_PROJ)
    per16 = TM_POST // HIST_ROWS
    h2_p, h2p_p, idx_p, gate_p = _post(
        dp, up, up, lambda i: (jnp.maximum(i * per16 - 1, 0), 0), xp2, mkb, mvb, wts,
        tm=TM_POST, seq=seq, pos0=0, zero_first_hist=True)

    xs2 = x_sample.reshape(ts, d)
    qs, ks, vs, us, ksb, vsb = _inproj(xs2, w_in_bf, TM_PROJ)
    ds = _diff_sample(qs, cache_diff_k.reshape(dec_batch * past_len, 1, DIFF_WIDTH),
                      cache_diff_v.reshape(dec_batch * past_len, 1, DIFF_WIDTH), ksb, vsb, lam_vecs, g_sub,
                      dec_batch, dec_seq, lambda_init)
    hist_s = jnp.pad(state_pool[0], ((0, 0), (1, 0), (0, 0))).reshape(dec_batch * HIST_ROWS, POOL_WIDTH)
    h2_s, h2p_s, idx_s, gate_s = _post(
        ds, us, hist_s, lambda i: (i, 0), xs2,
        cache_mem_k[0].reshape(dec_batch * N_MEM, d).astype(BF16),
        cache_mem_v[0].reshape(dec_batch * N_MEM, d).astype(BF16), wts,
        tm=dec_seq, seq=dec_seq, pos0=past_len, zero_first_hist=False)

    t_all = tp + ts
    idx_all = jnp.concatenate([_tokens_major(idx_p), _tokens_major(idx_s)], axis=1)
    gate_all = jnp.concatenate([_tokens_major(gate_p), _tokens_major(gate_s)], axis=1)
    rank_all, counts = _ranks(idx_all)
    counts = counts[:, 0]
    padded = (counts + MOE_BLOCK - 1) // MOE_BLOCK * MOE_BLOCK
    pad_end = jnp.cumsum(padded)
    pad_start = (pad_end - padded).astype(I32)
    n_blocks = t_all * TOP_K // MOE_BLOCK + N_EXPERTS
    block_start = jnp.arange(n_blocks, dtype=I32) * MOE_BLOCK
    block_expert = jnp.minimum(jnp.sum(block_start[:, None] >= pad_end[None, :], axis=1), N_EXPERTS - 1).astype(I32)
    block_active = (block_start < pad_end[-1]).astype(I32)
    block_new = jnp.concatenate([jnp.ones((1,), I32), (block_expert[1:] != block_expert[:-1]).astype(I32)])
    dest_flat = _dests(pad_start, idx_all, rank_all).T.reshape(-1)
    xs = _dispatch(counts, pad_start, h2p_p, h2p_s, dest_flat, n_blocks * MOE_BLOCK)
    yb = _experts(block_expert, block_active, block_new, xs,
                  moe_w_gate[0], moe_b_gate[0].reshape(N_EXPERTS, 1, D_MODEL),
                  moe_w_up[0], moe_b_up[0].reshape(N_EXPERTS, 1, D_MODEL),
                  moe_w_down[0], moe_b_down[0].reshape(N_EXPERTS, 1, D_MODEL))
    y_p, y_s = _combine(h2_p, h2_s, gate_all.T, dest_flat, yb, vec(ln3_g), vec(ln3_b))

    heads = (N_DIFF_HEADS, DIFF_V_DIM)
    xa = (N_XA_HEADS, XA_HEAD_DIM)
    up3 = up.reshape(batch, seq, POOL_WIDTH)
    us3 = us.reshape(dec_batch, dec_seq, POOL_WIDTH)
    pool_s = jnp.concatenate([state_pool[0].astype(F32), us3], axis=1)[:, -POOL_HIST:]
    return (y_p.reshape(batch, seq, d), y_s.reshape(dec_batch, dec_seq, d),
            kp.reshape(1, batch, seq, *heads), vp.reshape(1, batch, seq, *heads),
            up3[:, seq - POOL_HIST:][None],
            mk.reshape(1, batch, N_MEM, *xa), mv.reshape(1, batch, N_MEM, *xa),
            ks.reshape(1, dec_batch, dec_seq, *heads), vs.reshape(1, dec_batch, dec_seq, *heads),
            pool_s[None])
```
